```python
import math
import jax, jax.numpy as jnp
from jax import lax
import numpy as np

D_MODEL = 1024
BATCH = 4
SEQ = 4096
DEPTH = 4

GRID_W = 64
CTX_LEN = 256
N_MIXERS = 3
LAYER_KINDS = tuple(i % N_MIXERS for i in range(DEPTH))
KIND_SLOT = tuple(LAYER_KINDS[:i].count(LAYER_KINDS[i]) for i in range(DEPTH))
N_HYENA = LAYER_KINDS.count(0)
N_ATTN = LAYER_KINDS.count(1)
N_CHUNK = LAYER_KINDS.count(2)
LAST_CTX_LAYER = max([i for i, k in enumerate(LAYER_KINDS) if k == 1], default=-1)

HY_ORDER = 2
HY_SHORT = 3
HY_BANDS = 16
HY_EMB_DIM = 1 + 2 * HY_BANDS
HY_FILTER_HIDDEN = 64
HY_DECAY_TARGET = 1e-2
HY_FAST_DECAY_PCT = 0.3
HY_SLOW_DECAY_PCT = 1.5
HY_MIN_DECAY = math.log(HY_DECAY_TARGET) / HY_SLOW_DECAY_PCT
HY_MAX_DECAY = math.log(HY_DECAY_TARGET) / HY_FAST_DECAY_PCT
HY_PROJ = (HY_ORDER + 1) * D_MODEL

HEAD_DIM = 64
N_Q_HEADS = D_MODEL // HEAD_DIM
N_KV_HEADS = 4
GQA_GROUP = N_Q_HEADS // N_KV_HEADS
Q_BLOCK = 128
ROPE_THETA = 10000.0
ATTN_SCALE = HEAD_DIM ** -0.5
Q_COLS = N_Q_HEADS * HEAD_DIM
QKV_COLS = (N_Q_HEADS + 2 * N_KV_HEADS) * HEAD_DIM

CHUNK = 128
GMLP_HALF = 2 * D_MODEL
GMLP_GROUPS = 16
GMLP_GROUP_W = GMLP_HALF // GMLP_GROUPS

N_EXPERTS = 16
EXPERT_FF = 2 * D_MODEL
CAPACITY_FACTOR = 2

RMS_EPS = 1e-6
LN_EPS = 1e-5
DEEPNORM_ALPHA = (2 * DEPTH) ** 0.25
DEEPNORM_BETA = (8 * DEPTH) ** -0.25

kernel_name = 'hybrid_hyena_gqa_gmlp_ecmoe_diffusion'

F32 = jnp.float32


def layer_norm(x, g, b):
    xf = x.astype(F32)
    mu = jnp.mean(xf, axis=-1, keepdims=True)
    var = jnp.mean(jnp.square(xf - mu), axis=-1, keepdims=True)
    return ((xf - mu) * lax.rsqrt(var + LN_EPS) * g.astype(F32) + b.astype(F32)).astype(x.dtype)


def rms_norm(x, g):
    xf = x.astype(F32)
    return (xf * lax.rsqrt(jnp.mean(jnp.square(xf), axis=-1, keepdims=True) + RMS_EPS) * g.astype(F32)).astype(x.dtype)


def adaln(cond, w_mod, b_mod):
    m = jax.nn.silu(cond) @ w_mod + b_mod
    return jnp.split(m, 6, axis=-1)


def modulate(x, shift, scale):
    return x * (1.0 + scale) + shift


def post_norm(x, y, g, b):
    return layer_norm(DEEPNORM_ALPHA * x + y, g, b)


def centred_depthwise_conv(x, w, b):
    k = w.shape[0]
    y = lax.conv_general_dilated(x, w[:, None, :].astype(x.dtype), window_strides=(1,),
                                 padding=[(k // 2, k // 2)], dimension_numbers=('NWC', 'WIO', 'NWC'),
                                 feature_group_count=x.shape[-1])
    return y + b


def hyena_filters(L, f_w1, f_b1, f_w2, f_b2, f_w3, f_freq):
    t = jnp.linspace(0.0, 1.0, L, dtype=F32)[:, None]
    w = (2.0 * math.pi / L) * jnp.arange(L, dtype=F32)[:, None]
    f = jnp.linspace(1e-4, HY_BANDS - 1, HY_BANDS, dtype=F32)[None, :]
    z = jnp.concatenate([t, jnp.cos(f * w), -jnp.sin(f * w)], axis=-1)
    freq = f_freq.astype(F32)
    h = jnp.sin(freq[0] * (z @ f_w1.astype(F32) + f_b1.astype(F32)))
    h = jnp.sin(freq[1] * (h @ f_w2.astype(F32) + f_b2.astype(F32)))
    h = (h @ f_w3.astype(F32)).reshape(L, HY_ORDER, 2, D_MODEL)
    deltas = jnp.abs(jnp.linspace(HY_MIN_DECAY, HY_MAX_DECAY, D_MODEL, dtype=F32))
    h = h * jnp.exp(-t * deltas)[:, None, None, :]
    fwd, bwd = h[:, :, 0], h[:, :, 1]
    k2 = jnp.concatenate([fwd, jnp.zeros((1, HY_ORDER, D_MODEL), F32), bwd[:0:-1]], axis=0)
    k2 = k2 / (jnp.sum(jnp.abs(k2), axis=0, keepdims=True) + RMS_EPS)
    return jnp.fft.rfft(k2, axis=0)


def long_conv(v, k_f, bias):
    L = v.shape[1]
    vf32 = v.astype(F32)
    y = jnp.fft.irfft(jnp.fft.rfft(vf32, n=2 * L, axis=1) * k_f[None], n=2 * L, axis=1)[:, :L]
    return (y + vf32 * bias.astype(F32)).astype(v.dtype)


def hyena_mixer(h, w_in, conv_w, conv_b, f_w1, f_b1, f_w2, f_b2, f_w3, f_freq, f_bias, w_out):
    L = h.shape[1]
    p = centred_depthwise_conv(h @ w_in, conv_w, conv_b)
    v, x1, x2 = jnp.split(p, 3, axis=-1)
    k_f = hyena_filters(L, f_w1, f_b1, f_w2, f_b2, f_w3, f_freq)
    z = v
    for n, gate in enumerate((x1, x2)):
        z = gate * long_conv(z, k_f[:, n], f_bias[n])
    return z @ w_out


def axial_rope_tables(L):
    rows = L // GRID_W
    row = jnp.broadcast_to(jnp.arange(rows, dtype=F32)[:, None], (rows, GRID_W)).reshape(L)
    col = jnp.broadcast_to(jnp.arange(GRID_W, dtype=F32)[None, :], (rows, GRID_W)).reshape(L)
    n_freq = HEAD_DIM // 4
    inv_freq = ROPE_THETA ** (-jnp.arange(n_freq, dtype=F32) / n_freq)
    ang = jnp.concatenate([row[:, None] * inv_freq, col[:, None] * inv_freq], axis=-1)
    return jnp.cos(ang), jnp.sin(ang)


def apply_rope(x, cos, sin):
    shape = (1, x.shape[1]) + (1,) * (x.ndim - 3) + (HEAD_DIM // 2,)
    cos, sin = cos.reshape(shape), sin.reshape(shape)
    xf = x.astype(F32)
    x1, x2 = xf[..., :HEAD_DIM // 2], xf[..., HEAD_DIM // 2:]
    return jnp.concatenate([x1 * cos - x2 * sin, x1 * sin + x2 * cos], axis=-1).astype(x.dtype)


def q_heads(h, w_qkv, q_gain):
    B, L, _ = h.shape
    q = (h @ w_qkv[:, :Q_COLS]).reshape(B, L, N_KV_HEADS, GQA_GROUP, HEAD_DIM)
    return rms_norm(q, q_gain)


def kv_heads(h, w_qkv, k_gain):
    B, L, _ = h.shape
    k, v = jnp.split(h @ w_qkv[:, Q_COLS:], 2, axis=-1)
    return rms_norm(k.reshape(B, L, N_KV_HEADS, HEAD_DIM), k_gain), v.reshape(B, L, N_KV_HEADS, HEAD_DIM)


def block_attention(q, keys, vals):
    B, Lq = q.shape[:2]
    qb = jnp.moveaxis(q.reshape((B, Lq // Q_BLOCK, Q_BLOCK) + q.shape[2:]), 1, 0)

    def one_block(qi):
        s = jnp.einsum('bqkgd,bskd->bkgqs', qi, keys, preferred_element_type=F32) * ATTN_SCALE
        p = jax.nn.softmax(s, axis=-1).astype(vals.dtype)
        return jnp.einsum('bkgqs,bskd->bqkgd', p, vals)

    o = lax.map(one_block, qb)
    return jnp.moveaxis(o, 0, 1).reshape(B, Lq, Q_COLS)


def chunk_gmlp_mixer(h, w_in, ln_g, ln_b, w_s, b_s, w_out):
    B, L, _ = h.shape
    z = jax.nn.gelu(h @ w_in, approximate=False)
    u, v = jnp.split(z, 2, axis=-1)
    v = layer_norm(v, ln_g, ln_b).reshape(B, L // CHUNK, CHUNK, GMLP_GROUPS, GMLP_GROUP_W)
    v = jnp.einsum('gts,bnsgc->bntgc', w_s, v) + b_s.T[None, None, :, :, None]
    return (u * v.reshape(B, L, GMLP_HALF)) @ w_out


def expert_choice_ffn(h, w_router, w_gate, w_up, w_down):
    B, n, _ = h.shape
    cap = CAPACITY_FACTOR * n // N_EXPERTS
    aff = jax.nn.softmax((h @ w_router).astype(F32), axis=-1)
    gate, idx = lax.top_k(jnp.swapaxes(aff, 1, 2), cap)
    bidx = jnp.arange(B)[:, None, None]
    xg = h[bidx, idx]
    a = jnp.einsum('becd,edf->becf', xg, w_gate)
    u = jnp.einsum('becd,edf->becf', xg, w_up)
    y = jnp.einsum('becf,efd->becd', jax.nn.silu(a) * u, w_down) * gate[..., None].astype(h.dtype)
    return jnp.zeros_like(h).at[bidx, idx].add(y)


def setup_inputs(seed: int = 0) -> dict:
    key = jax.random.key(seed)
    ks = jax.random.split(key, 33)
    D = D_MODEL

    def nrm(i, shape, scale):
        return scale * jax.random.normal(ks[i], shape, F32)

    return {
        'x': nrm(0, (BATCH, SEQ, D), 1.0),
        'c': nrm(1, (BATCH, D), 1.0),
        'ctx': nrm(2, (BATCH, CTX_LEN, D), 1.0),
        'c_ctx': nrm(3, (D,), 1.0),
        'mod_w': nrm(4, (DEPTH, D, 6 * D), D ** -0.5),
        'mod_b': nrm(5, (DEPTH, 6 * D), 0.02),
        'ln_g': 1.0 + nrm(6, (DEPTH, 2, D), 0.02),
        'ln_b': nrm(7, (DEPTH, 2, D), 0.02),
        'hy_w_in': nrm(8, (N_HYENA, D, HY_PROJ), D ** -0.5),
        'hy_conv_w': nrm(9, (N_HYENA, HY_SHORT, HY_PROJ), HY_SHORT ** -0.5),
        'hy_conv_b': nrm(10, (N_HYENA, HY_PROJ), 0.02),
        'hy_f_w1': nrm(11, (N_HYENA, HY_EMB_DIM, HY_FILTER_HIDDEN), HY_EMB_DIM ** -0.5),
        'hy_f_b1': nrm(12, (N_HYENA, HY_FILTER_HIDDEN), 0.02),
        'hy_f_w2': nrm(13, (N_HYENA, HY_FILTER_HIDDEN, HY_FILTER_HIDDEN), HY_FILTER_HIDDEN ** -0.5),
        'hy_f_b2': nrm(14, (N_HYENA, HY_FILTER_HIDDEN), 0.02),
        'hy_f_w3': nrm(15, (N_HYENA, HY_FILTER_HIDDEN, HY_ORDER * 2 * D), HY_FILTER_HIDDEN ** -0.5),
        'hy_f_freq': 1.0 + nrm(16, (N_HYENA, 2, HY_FILTER_HIDDEN), 0.02),
        'hy_f_bias': nrm(17, (N_HYENA, HY_ORDER, D), 0.02),
        'hy_w_out': nrm(18, (N_HYENA, D, D), DEEPNORM_BETA * D ** -0.5),
        'at_w_qkv': nrm(19, (N_ATTN, D, QKV_COLS), D ** -0.5),
        'at_q_gain': 1.0 + nrm(20, (N_ATTN, HEAD_DIM), 0.02),
        'at_k_gain': 1.0 + nrm(21, (N_ATTN, HEAD_DIM), 0.02),
        'at_w_out': nrm(22, (N_ATTN, Q_COLS, D), DEEPNORM_BETA * Q_COLS ** -0.5),
        'cm_w_in': nrm(23, (N_CHUNK, D, 2 * GMLP_HALF), D ** -0.5),
        'cm_ln_g': 1.0 + nrm(24, (N_CHUNK, GMLP_HALF), 0.02),
        'cm_ln_b': nrm(25, (N_CHUNK, GMLP_HALF), 0.02),
        'cm_w_s': nrm(26, (N_CHUNK, GMLP_GROUPS, CHUNK, CHUNK), CHUNK ** -0.5),
        'cm_b_s': 1.0 + nrm(27, (N_CHUNK, GMLP_GROUPS, CHUNK), 0.02),
        'cm_w_out': nrm(28, (N_CHUNK, GMLP_HALF, D), DEEPNORM_BETA * GMLP_HALF ** -0.5),
        'moe_router': nrm(29, (DEPTH, D, N_EXPERTS), D ** -0.5),
        'moe_w_gate': nrm(30, (DEPTH, N_EXPERTS, D, EXPERT_FF), D ** -0.5),
        'moe_w_up': nrm(31, (DEPTH, N_EXPERTS, D, EXPERT_FF), D ** -0.5),
        'moe_w_down': nrm(32, (DEPTH, N_EXPERTS, EXPERT_FF, D), DEEPNORM_BETA * EXPERT_FF ** -0.5),
    }


def reference(x, c, ctx, c_ctx, mod_w, mod_b, ln_g, ln_b,
              hy_w_in, hy_conv_w, hy_conv_b, hy_f_w1, hy_f_b1, hy_f_w2, hy_f_b2, hy_f_w3, hy_f_freq, hy_f_bias, hy_w_out,
              at_w_qkv, at_q_gain, at_k_gain, at_w_out,
              cm_w_in, cm_ln_g, cm_ln_b, cm_w_s, cm_b_s, cm_w_out,
              moe_router, moe_w_gate, moe_w_up, moe_w_down):
    x_lat, x_ctx = x, ctx
    for i in range(DEPTH):
        kind, slot = LAYER_KINDS[i], KIND_SLOT[i]
        ctx_read = i <= LAST_CTX_LAYER
        ctx_update = i < LAST_CTX_LAYER
        sh1, sc1, g1, sh2, sc2, g2 = adaln(c[:, None, :], mod_w[i], mod_b[i])
        h = modulate(x_lat, sh1, sc1)
        if ctx_read:
            cmod = adaln(c_ctx, mod_w[i], mod_b[i])
            hc = modulate(x_ctx, cmod[0], cmod[1])

        if kind == 0:
            hy = (hy_w_in[slot], hy_conv_w[slot], hy_conv_b[slot], hy_f_w1[slot], hy_f_b1[slot], hy_f_w2[slot],
                  hy_f_b2[slot], hy_f_w3[slot], hy_f_freq[slot], hy_f_bias[slot], hy_w_out[slot])
            y = hyena_mixer(h, *hy)
            if ctx_update:
                yc = hyena_mixer(hc, *hy)
        elif kind == 1:
            w_qkv = at_w_qkv[slot]
            ck, cv = kv_heads(hc, w_qkv, at_k_gain[slot])
            cos, sin = axial_rope_tables(h.shape[1])
            q = apply_rope(q_heads(h, w_qkv, at_q_gain[slot]), cos, sin)
            k, v = kv_heads(h, w_qkv, at_k_gain[slot])
            k = apply_rope(k, cos, sin)
            o = block_attention(q, jnp.concatenate([ck, k], axis=1), jnp.concatenate([cv, v], axis=1))
            y = o @ at_w_out[slot]
            if ctx_update:
                yc = block_attention(q_heads(hc, w_qkv, at_q_gain[slot]), ck, cv) @ at_w_out[slot]
        else:
            cm = (cm_w_in[slot], cm_ln_g[slot], cm_ln_b[slot], cm_w_s[slot], cm_b_s[slot], cm_w_out[slot])
            y = chunk_gmlp_mixer(h, *cm)
            if ctx_update:
                yc = chunk_gmlp_mixer(hc, *cm)

        moe = (moe_router[i], moe_w_gate[i], moe_w_up[i], moe_w_down[i])
        x_lat = post_norm(x_lat, g1 * y, ln_g[i, 0], ln_b[i, 0])
        x_lat = post_norm(x_lat, g2 * expert_choice_ffn(modulate(x_lat, sh2, sc2), *moe), ln_g[i, 1], ln_b[i, 1])
        if ctx_update:
            x_ctx = post_norm(x_ctx, cmod[2] * yc, ln_g[i, 0], ln_b[i, 0])
            x_ctx = post_norm(x_ctx, cmod[5] * expert_choice_ffn(modulate(x_ctx, cmod[3], cmod[4]), *moe),
                              ln_g[i, 1], ln_b[i, 1])
    return x_lat
```

```python
import math
from functools import partial

import jax
import jax.numpy as jnp
from jax import lax
from jax.experimental import pallas as pl
from jax.experimental.pallas import tpu as pltpu

F32 = jnp.float32
BF16 = jnp.bfloat16

D_MODEL = 1024
DEPTH = 4
GRID_W = 64
N_MIXERS = 3
LAYER_KINDS = tuple(i % N_MIXERS for i in range(DEPTH))
KIND_SLOT = tuple(LAYER_KINDS[:i].count(LAYER_KINDS[i]) for i in range(DEPTH))
LAST_CTX_LAYER = max([i for i, k in enumerate(LAYER_KINDS) if k == 1], default=-1)

HY_ORDER = 2
HY_BANDS = 16
HY_DECAY_TARGET = 1e-2
HY_FAST_DECAY_PCT = 0.3
HY_SLOW_DECAY_PCT = 1.5
HY_MIN_DECAY = math.log(HY_DECAY_TARGET) / HY_SLOW_DECAY_PCT
HY_MAX_DECAY = math.log(HY_DECAY_TARGET) / HY_FAST_DECAY_PCT

HEAD_DIM = 64
N_Q_HEADS = D_MODEL // HEAD_DIM
N_KV_HEADS = 4
GQA_GROUP = N_Q_HEADS // N_KV_HEADS
ROPE_THETA = 10000.0
ATTN_SCALE = HEAD_DIM ** -0.5
Q_COLS = N_Q_HEADS * HEAD_DIM

CHUNK = 128
GMLP_HALF = 2 * D_MODEL
GMLP_GROUPS = 16
GMLP_GROUP_W = GMLP_HALF // GMLP_GROUPS

N_EXPERTS = 16
CAPACITY_FACTOR = 2

RMS_EPS = 1e-6
LN_EPS = 1e-5
DEEPNORM_ALPHA = (2 * DEPTH) ** 0.25

VMEM_LIMIT_BYTES = 56 * 1024 * 1024
OUT_TILE_BYTES = 4 * 1024 * 1024


def _row_tile(rows, n_cols):
    tm = 512
    while tm > 8 and (tm * n_cols * 4 > OUT_TILE_BYTES or rows % tm):
        tm //= 2
    return tm


def _params(sem):
    return pltpu.CompilerParams(dimension_semantics=sem, vmem_limit_bytes=VMEM_LIMIT_BYTES)


def _mod_matmul_kernel(x_ref, sh_ref, sc_ref, w_ref, o_ref):
    h = x_ref[0] * (1.0 + sc_ref[0]) + sh_ref[0]
    o_ref[0] = jnp.dot(h.astype(BF16), w_ref[...], preferred_element_type=F32)


def mod_matmul(x, shift, scale, w):
    B, L, K = x.shape
    N = w.shape[1]
    tm = _row_tile(L, N)
    return pl.pallas_call(
        _mod_matmul_kernel,
        grid=(B, L // tm),
        in_specs=[
            pl.BlockSpec((1, tm, K), lambda b, i: (b, i, 0)),
            pl.BlockSpec((1, 1, K), lambda b, i: (b, 0, 0)),
            pl.BlockSpec((1, 1, K), lambda b, i: (b, 0, 0)),
            pl.BlockSpec((K, N), lambda b, i: (0, 0)),
        ],
        out_specs=pl.BlockSpec((1, tm, N), lambda b, i: (b, i, 0)),
        out_shape=jax.ShapeDtypeStruct((B, L, N), F32),
        compiler_params=_params(("parallel", "parallel")),
        name="mod_matmul",
    )(x, shift, scale, w)


def _layer_norm_rows(t, g, b):
    mu = jnp.mean(t, axis=-1, keepdims=True)
    d = t - mu
    var = jnp.mean(d * d, axis=-1, keepdims=True)
    return d * lax.rsqrt(var + LN_EPS) * g + b


def _matmul_postnorm_kernel(z_ref, w_ref, x_ref, gate_ref, g_ref, b_ref, o_ref):
    y = jnp.dot(z_ref[0].astype(BF16), w_ref[...], preferred_element_type=F32)
    t = DEEPNORM_ALPHA * x_ref[0] + gate_ref[0] * y
    o_ref[0] = _layer_norm_rows(t, g_ref[...], b_ref[...])


def matmul_postnorm(z, w, x, gate, g, b):
    B, L, K = z.shape
    D = w.shape[1]
    tm = _row_tile(L, max(K, D))
    return pl.pallas_call(
        _matmul_postnorm_kernel,
        grid=(B, L // tm),
        in_specs=[
            pl.BlockSpec((1, tm, K), lambda bi, i: (bi, i, 0)),
            pl.BlockSpec((K, D), lambda bi, i: (0, 0)),
            pl.BlockSpec((1, tm, D), lambda bi, i: (bi, i, 0)),
            pl.BlockSpec((1, 1, D), lambda bi, i: (bi, 0, 0)),
            pl.BlockSpec((1, D), lambda bi, i: (0, 0)),
            pl.BlockSpec((1, D), lambda bi, i: (0, 0)),
        ],
        out_specs=pl.BlockSpec((1, tm, D), lambda bi, i: (bi, i, 0)),
        out_shape=jax.ShapeDtypeStruct((B, L, D), F32),
        compiler_params=_params(("parallel", "parallel")),
        name="matmul_postnorm",
    )(z, w, x, gate, g, b)


def _postnorm_kernel(y_ref, x_ref, gate_ref, g_ref, b_ref, o_ref):
    t = DEEPNORM_ALPHA * x_ref[0] + gate_ref[0] * y_ref[0]
    o_ref[0] = _layer_norm_rows(t, g_ref[...], b_ref[...])


def postnorm(y, x, gate, g, b):
    B, L, D = x.shape
    tm = _row_tile(L, D)
    return pl.pallas_call(
        _postnorm_kernel,
        grid=(B, L // tm),
        in_specs=[
            pl.BlockSpec((1, tm, D), lambda bi, i: (bi, i, 0)),
            pl.BlockSpec((1, tm, D), lambda bi, i: (bi, i, 0)),
            pl.BlockSpec((1, 1, D), lambda bi, i: (bi, 0, 0)),
            pl.BlockSpec((1, D), lambda bi, i: (0, 0)),
            pl.BlockSpec((1, D), lambda bi, i: (0, 0)),
        ],
        out_specs=pl.BlockSpec((1, tm, D), lambda bi, i: (bi, i, 0)),
        out_shape=jax.ShapeDtypeStruct((B, L, D), F32),
        compiler_params=_params(("parallel", "parallel")),
        name="postnorm",
    )(y, x, gate, g, b)


def _attention_kernel(q_ref, kt_ref, v_ref, o_ref):
    g, tq, hd = q_ref.shape[1], q_ref.shape[2], q_ref.shape[3]
    q = q_ref[0].reshape(g * tq, hd)
    s = jnp.dot(q, kt_ref[0, 0], preferred_element_type=F32)
    m = jnp.max(s, axis=-1, keepdims=True)
    p = jnp.exp(s - m)
    l = jnp.sum(p, axis=-1, keepdims=True)
    o = jnp.dot(p.astype(BF16), v_ref[0, 0], preferred_element_type=F32)
    o_ref[0] = (o / l).reshape(g, tq, hd)


def attention(q, kt, v, tq):
    B, H, Lq, hd = q.shape
    KV, Lk = kt.shape[1], kt.shape[3]
    G = H // KV
    return pl.pallas_call(
        _attention_kernel,
        grid=(B, KV, Lq // tq),
        in_specs=[
            pl.BlockSpec((1, G, tq, hd), lambda b, k, i: (b, k, i, 0)),
            pl.BlockSpec((1, 1, hd, Lk), lambda b, k, i: (b, k, 0, 0)),
            pl.BlockSpec((1, 1, Lk, hd), lambda b, k, i: (b, k, 0, 0)),
        ],
        out_specs=pl.BlockSpec((1, G, tq, hd), lambda b, k, i: (b, k, i, 0)),
        out_shape=jax.ShapeDtypeStruct((B, H, Lq, hd), F32),
        compiler_params=_params(("parallel", "parallel", "parallel")),
        name="gqa_attention",
    )(q, kt, v)


def _expert_ffn_kernel(x_ref, wg_ref, wu_ref, wd_ref, gate_ref, o_ref):
    f = pl.program_id(1)
    x = x_ref[0]
    a = jnp.dot(x, wg_ref[0].astype(BF16), preferred_element_type=F32)
    u = jnp.dot(x, wu_ref[0].astype(BF16), preferred_element_type=F32)
    h = (a * jax.nn.sigmoid(a) * u).astype(BF16)
    y = jnp.dot(h, wd_ref[0].astype(BF16), preferred_element_type=F32)

    @pl.when(f == 0)
    def _():
        o_ref[0] = y

    @pl.when(f > 0)
    def _():
        o_ref[0] += y

    @pl.when(f == pl.num_programs(1) - 1)
    def _():
        o_ref[0] *= gate_ref[0]


def expert_ffn(xg, w_gate, w_up, w_down, gate, tf):
    E, R, D = xg.shape
    F = w_gate.shape[2]
    return pl.pallas_call(
        _expert_ffn_kernel,
        grid=(E, F // tf),
        in_specs=[
            pl.BlockSpec((1, R, D), lambda e, f: (e, 0, 0)),
            pl.BlockSpec((1, D, tf), lambda e, f: (e, 0, f)),
            pl.BlockSpec((1, D, tf), lambda e, f: (e, 0, f)),
            pl.BlockSpec((1, tf, D), lambda e, f: (e, f, 0)),
            pl.BlockSpec((1, R, 1), lambda e, f: (e, 0, 0)),
        ],
        out_specs=pl.BlockSpec((1, R, D), lambda e, f: (e, 0, 0)),
        out_shape=jax.ShapeDtypeStruct((E, R, D), F32),
        compiler_params=_params(("parallel", "arbitrary")),
        name="expert_ffn",
    )(xg, w_gate, w_up, w_down, gate)


def _adaln(cond, w_mod, b_mod):
    m = jax.nn.silu(cond) @ w_mod + b_mod
    return jnp.split(m, 6, axis=-1)


def _rms_norm(x, g):
    return x * lax.rsqrt(jnp.mean(jnp.square(x), axis=-1, keepdims=True) + RMS_EPS) * g


def _centred_depthwise_conv(x, w, b):
    k = w.shape[0]
    y = lax.conv_general_dilated(x, w[:, None, :], window_strides=(1,), padding=[(k // 2, k // 2)],
                                 dimension_numbers=('NWC', 'WIO', 'NWC'), feature_group_count=x.shape[-1])
    return y + b


def _hyena_filters(L, f_w1, f_b1, f_w2, f_b2, f_w3, f_freq):
    t = jnp.linspace(0.0, 1.0, L, dtype=F32)[:, None]
    w = (2.0 * math.pi / L) * jnp.arange(L, dtype=F32)[:, None]
    f = jnp.linspace(1e-4, HY_BANDS - 1, HY_BANDS, dtype=F32)[None, :]
    z = jnp.concatenate([t, jnp.cos(f * w), -jnp.sin(f * w)], axis=-1)
    h = jnp.sin(f_freq[0] * (z @ f_w1 + f_b1))
    h = jnp.sin(f_freq[1] * (h @ f_w2 + f_b2))
    h = (h @ f_w3).reshape(L, HY_ORDER, 2, D_MODEL)
    deltas = jnp.abs(jnp.linspace(HY_MIN_DECAY, HY_MAX_DECAY, D_MODEL, dtype=F32))
    h = h * jnp.exp(-t * deltas)[:, None, None, :]
    fwd, bwd = h[:, :, 0], h[:, :, 1]
    k2 = jnp.concatenate([fwd, jnp.zeros((1, HY_ORDER, D_MODEL), F32), bwd[:0:-1]], axis=0)
    k2 = k2 / (jnp.sum(jnp.abs(k2), axis=0, keepdims=True) + RMS_EPS)
    return jnp.fft.rfft(k2, axis=0)


def _long_conv(v, k_f, bias):
    L = v.shape[1]
    y = jnp.fft.irfft(jnp.fft.rfft(v, n=2 * L, axis=1) * k_f[None], n=2 * L, axis=1)[:, :L]
    return y + v * bias


def _hyena_core(p, conv_w, conv_b, f_w1, f_b1, f_w2, f_b2, f_w3, f_freq, f_bias):
    L = p.shape[1]
    p = _centred_depthwise_conv(p, conv_w, conv_b)
    v, x1, x2 = jnp.split(p, 3, axis=-1)
    k_f = _hyena_filters(L, f_w1, f_b1, f_w2, f_b2, f_w3, f_freq)
    z = v
    for n, gate in enumerate((x1, x2)):
        z = gate * _long_conv(z, k_f[:, n], f_bias[n])
    return z


def _axial_rope_tables(L):
    rows = L // GRID_W
    row = jnp.broadcast_to(jnp.arange(rows, dtype=F32)[:, None], (rows, GRID_W)).reshape(L)
    col = jnp.broadcast_to(jnp.arange(GRID_W, dtype=F32)[None, :], (rows, GRID_W)).reshape(L)
    n_freq = HEAD_DIM // 4
    inv_freq = ROPE_THETA ** (-jnp.arange(n_freq, dtype=F32) / n_freq)
    ang = jnp.concatenate([row[:, None] * inv_freq, col[:, None] * inv_freq], axis=-1)
    return jnp.cos(ang), jnp.sin(ang)


def _apply_rope(x, cos, sin):
    shape = (1, x.shape[1]) + (1,) * (x.ndim - 3) + (HEAD_DIM // 2,)
    cos, sin = cos.reshape(shape), sin.reshape(shape)
    x1, x2 = x[..., :HEAD_DIM // 2], x[..., HEAD_DIM // 2:]
    return jnp.concatenate([x1 * cos - x2 * sin, x1 * sin + x2 * cos], axis=-1)


def _expert_choice_ffn(x, shift, scale, w_router, w_gate, w_up, w_down):
    B, n, D = x.shape
    cap = CAPACITY_FACTOR * n // N_EXPERTS
    h = x * (1.0 + scale) + shift
    aff = jax.nn.softmax(h @ w_router, axis=-1)
    gate, idx = lax.top_k(jnp.swapaxes(aff, 1, 2), cap)
    bidx = jnp.arange(B)[:, None, None]
    xg = h.astype(BF16)[bidx, idx]
    xg = jnp.swapaxes(xg, 0, 1).reshape(N_EXPERTS, B * cap, D)
    gt = jnp.swapaxes(gate, 0, 1).reshape(N_EXPERTS, B * cap, 1)
    y = expert_ffn(xg, w_gate, w_up, w_down, gt, tf=512)
    y = jnp.swapaxes(y.reshape(N_EXPERTS, B, cap, D), 0, 1)
    return jnp.zeros_like(x).at[bidx, idx].add(y)


def kernel(x, c, ctx, c_ctx, mod_w, mod_b, ln_g, ln_b, hy_w_in, hy_conv_w, hy_conv_b, hy_f_w1, hy_f_b1, hy_f_w2, hy_f_b2, hy_f_w3, hy_f_freq, hy_f_bias, hy_w_out, at_w_qkv, at_q_gain, at_k_gain, at_w_out, cm_w_in, cm_ln_g, cm_ln_b, cm_w_s, cm_b_s, cm_w_out, moe_router, moe_w_gate, moe_w_up, moe_w_down):
    B, L, D = x.shape
    Lc = ctx.shape[1]
    x_lat, x_ctx = x, ctx
    for i in range(DEPTH):
        kind, slot = LAYER_KINDS[i], KIND_SLOT[i]
        ctx_read = i <= LAST_CTX_LAYER
        ctx_update = i < LAST_CTX_LAYER
        sh1, sc1, g1, sh2, sc2, g2 = _adaln(c[:, None, :], mod_w[i], mod_b[i])
        if ctx_read:
            cmod = [jnp.broadcast_to(m[None, None, :], (B, 1, D)) for m in _adaln(c_ctx, mod_w[i], mod_b[i])]
        lg0, lb0, lg1, lb1 = ln_g[i, 0][None], ln_b[i, 0][None], ln_g[i, 1][None], ln_b[i, 1][None]

        if kind == 0:
            w_in = hy_w_in[slot].astype(BF16)
            w_out = hy_w_out[slot].astype(BF16)
            hy = (hy_conv_w[slot], hy_conv_b[slot], hy_f_w1[slot], hy_f_b1[slot], hy_f_w2[slot], hy_f_b2[slot],
                  hy_f_w3[slot], hy_f_freq[slot], hy_f_bias[slot])
            z = _hyena_core(mod_matmul(x_lat, sh1, sc1, w_in), *hy)
            x_lat = matmul_postnorm(z, w_out, x_lat, g1, lg0, lb0)
            if ctx_update:
                zc = _hyena_core(mod_matmul(x_ctx, cmod[0], cmod[1], w_in), *hy)
                x_ctx = matmul_postnorm(zc, w_out, x_ctx, cmod[2], lg0, lb0)
        elif kind == 1:
            w_qkv = at_w_qkv[slot].astype(BF16)
            w_out = at_w_out[slot].astype(BF16)
            cos, sin = _axial_rope_tables(L)
            qkv = mod_matmul(x_lat, sh1, sc1, w_qkv)
            q = _rms_norm(qkv[..., :Q_COLS].reshape(B, L, N_Q_HEADS, HEAD_DIM), at_q_gain[slot])
            q = _apply_rope(q, cos, sin) * ATTN_SCALE
            k, v = jnp.split(qkv[..., Q_COLS:], 2, axis=-1)
            k = _apply_rope(_rms_norm(k.reshape(B, L, N_KV_HEADS, HEAD_DIM), at_k_gain[slot]), cos, sin)
            v = v.reshape(B, L, N_KV_HEADS, HEAD_DIM)
            kvc = mod_matmul(x_ctx, cmod[0], cmod[1], w_qkv[:, Q_COLS:])
            ck, cv = jnp.split(kvc, 2, axis=-1)
            ck = _rms_norm(ck.reshape(B, Lc, N_KV_HEADS, HEAD_DIM), at_k_gain[slot])
            cv = cv.reshape(B, Lc, N_KV_HEADS, HEAD_DIM)
            keys = jnp.concatenate([ck, k], axis=1).astype(BF16)
            vals = jnp.concatenate([cv, v], axis=1).astype(BF16)
            o = attention(jnp.transpose(q.astype(BF16), (0, 2, 1, 3)),
                          jnp.transpose(keys, (0, 2, 3, 1)),
                          jnp.transpose(vals, (0, 2, 1, 3)), tq=128)
            o = jnp.transpose(o, (0, 2, 1, 3)).reshape(B, L, Q_COLS)
            x_lat = matmul_postnorm(o, w_out, x_lat, g1, lg0, lb0)
        else:
            w_in = cm_w_in[slot].astype(BF16)
            w_out = cm_w_out[slot].astype(BF16)
            zz = jax.nn.gelu(mod_matmul(x_lat, sh1, sc1, w_in), approximate=False)
            u, v = jnp.split(zz, 2, axis=-1)
            mu = jnp.mean(v, axis=-1, keepdims=True)
            var = jnp.mean(jnp.square(v - mu), axis=-1, keepdims=True)
            v = (v - mu) * lax.rsqrt(var + LN_EPS) * cm_ln_g[slot] + cm_ln_b[slot]
            v = v.reshape(B, L // CHUNK, CHUNK, GMLP_GROUPS, GMLP_GROUP_W)
            v = jnp.einsum('gts,bnsgc->bntgc', cm_w_s[slot], v) + cm_b_s[slot].T[None, None, :, :, None]
            x_lat = matmul_postnorm(u * v.reshape(B, L, GMLP_HALF), w_out, x_lat, g1, lg0, lb0)

        moe = (moe_router[i], moe_w_gate[i], moe_w_up[i], moe_w_down[i])
        x_lat = postnorm(_expert_choice_ffn(x_lat, sh2, sc2, *moe), x_lat, g2, lg1, lb1)
        if ctx_update:
            x_ctx = postnorm(_expert_choice_ffn(x_ctx, cmod[3], cmod[4], *moe), x_ctx, cmod[5], lg1, lb1)
    return x_lat
```

```python
import math
from functools import partial

import jax
import jax.numpy as jnp
from jax import lax
from jax.experimental import pallas as pl
from jax.experimental.pallas import tpu as pltpu

F32 = jnp.float32
BF16 = jnp.bfloat16

D_MODEL = 1024
DEPTH = 4
GRID_W = 64
N_MIXERS = 3
LAYER_KINDS = tuple(i % N_MIXERS for i in range(DEPTH))
KIND_SLOT = tuple(LAYER_KINDS[:i].count(LAYER_KINDS[i]) for i in range(DEPTH))
LAST_CTX_LAYER = max([i for i, k in enumerate(LAYER_KINDS) if k == 1], default=-1)

HY_ORDER = 2
HY_BANDS = 16
HY_DECAY_TARGET = 1e-2
HY_FAST_DECAY_PCT = 0.3
HY_SLOW_DECAY_PCT = 1.5
HY_MIN_DECAY = math.log(HY_DECAY_TARGET) / HY_SLOW_DECAY_PCT
HY_MAX_DECAY = math.log(HY_DECAY_TARGET) / HY_FAST_DECAY_PCT

HEAD_DIM = 64
N_Q_HEADS = D_MODEL // HEAD_DIM
N_KV_HEADS = 4
GQA_GROUP = N_Q_HEADS // N_KV_HEADS
ROPE_THETA = 10000.0
ATTN_SCALE = HEAD_DIM ** -0.5
Q_COLS = N_Q_HEADS * HEAD_DIM

CHUNK = 128
GMLP_HALF = 2 * D_MODEL
GMLP_GROUPS = 16
GMLP_GROUP_W = GMLP_HALF // GMLP_GROUPS

N_EXPERTS = 16
CAPACITY_FACTOR = 2

RMS_EPS = 1e-6
LN_EPS = 1e-5
DEEPNORM_ALPHA = (2 * DEPTH) ** 0.25

VMEM_LIMIT_BYTES = 56 * 1024 * 1024
OUT_TILE_BYTES = 4 * 1024 * 1024


def _row_tile(rows, n_cols):
    tm = 512
    while tm > 8 and (tm * n_cols * 4 > OUT_TILE_BYTES or rows % tm):
        tm //= 2
    return tm


def _params(sem):
    return pltpu.CompilerParams(dimension_semantics=sem, vmem_limit_bytes=VMEM_LIMIT_BYTES)


def _mod_matmul_kernel(x_ref, sh_ref, sc_ref, w_ref, o_ref):
    h = x_ref[0] * (1.0 + sc_ref[0]) + sh_ref[0]
    o_ref[0] = jnp.dot(h.astype(BF16), w_ref[...], preferred_element_type=F32)


def mod_matmul(x, shift, scale, w):
    B, L, K = x.shape
    N = w.shape[1]
    tm = _row_tile(L, N)
    return pl.pallas_call(
        _mod_matmul_kernel,
        grid=(B, L // tm),
        in_specs=[
            pl.BlockSpec((1, tm, K), lambda b, i: (b, i, 0)),
            pl.BlockSpec((1, 1, K), lambda b, i: (b, 0, 0)),
            pl.BlockSpec((1, 1, K), lambda b, i: (b, 0, 0)),
            pl.BlockSpec((K, N), lambda b, i: (0, 0)),
        ],
        out_specs=pl.BlockSpec((1, tm, N), lambda b, i: (b, i, 0)),
        out_shape=jax.ShapeDtypeStruct((B, L, N), F32),
        compiler_params=_params(("parallel", "parallel")),
        name="mod_matmul",
    )(x, shift, scale, w)


def _layer_norm_rows(t, g, b):
    mu = jnp.mean(t, axis=-1, keepdims=True)
    d = t - mu
    var = jnp.mean(d * d, axis=-1, keepdims=True)
    return d * lax.rsqrt(var + LN_EPS) * g + b


def _matmul_postnorm_kernel(z_ref, w_ref, x_ref, gate_ref, g_ref, b_ref, o_ref):
    y = jnp.dot(z_ref[0].astype(BF16), w_ref[...], preferred_element_type=F32)
    t = DEEPNORM_ALPHA * x_ref[0] + gate_ref[0] * y
    o_ref[0] = _layer_norm_rows(t, g_ref[...], b_ref[...])


def matmul_postnorm(z, w, x, gate, g, b):
    B, L, K = z.shape
    D = w.shape[1]
    tm = _row_tile(L, max(K, D))
    return pl.pallas_call(
        _matmul_postnorm_kernel,
        grid=(B, L // tm),
        in_specs=[
            pl.BlockSpec((1, tm, K), lambda bi, i: (bi, i, 0)),
            pl.BlockSpec((K, D), lambda bi, i: (0, 0)),
            pl.BlockSpec((1, tm, D), lambda bi, i: (bi, i, 0)),
            pl.BlockSpec((1, 1, D), lambda bi, i: (bi, 0, 0)),
            pl.BlockSpec((1, D), lambda bi, i: (0, 0)),
            pl.BlockSpec((1, D), lambda bi, i: (0, 0)),
        ],
        out_specs=pl.BlockSpec((1, tm, D), lambda bi, i: (bi, i, 0)),
        out_shape=jax.ShapeDtypeStruct((B, L, D), F32),
        compiler_params=_params(("parallel", "parallel")),
        name="matmul_postnorm",
    )(z, w, x, gate, g, b)


def _postnorm_kernel(y_ref, x_ref, gate_ref, g_ref, b_ref, o_ref):
    t = DEEPNORM_ALPHA * x_ref[0] + gate_ref[0] * y_ref[0]
    o_ref[0] = _layer_norm_rows(t, g_ref[...], b_ref[...])


def postnorm(y, x, gate, g, b):
    B, L, D = x.shape
    tm = _row_tile(L, D)
    return pl.pallas_call(
        _postnorm_kernel,
        grid=(B, L // tm),
        in_specs=[
            pl.BlockSpec((1, tm, D), lambda bi, i: (bi, i, 0)),
            pl.BlockSpec((1, tm, D), lambda bi, i: (bi, i, 0)),
            pl.BlockSpec((1, 1, D), lambda bi, i: (bi, 0, 0)),
            pl.BlockSpec((1, D), lambda bi, i: (0, 0)),
            pl.BlockSpec((1, D), lambda bi, i: (0, 0)),
        ],
        out_specs=pl.BlockSpec((1, tm, D), lambda bi, i: (bi, i, 0)),
        out_shape=jax.ShapeDtypeStruct((B, L, D), F32),
        compiler_params=_params(("parallel", "parallel")),
        name="postnorm",
    )(y, x, gate, g, b)


def _attention_kernel(q_ref, kt_ref, v_ref, o_ref):
    g, tq, hd = q_ref.shape[1], q_ref.shape[2], q_ref.shape[3]
    q = q_ref[0].reshape(g * tq, hd)
    s = jnp.dot(q, kt_ref[0, 0], preferred_element_type=F32)
    m = jnp.max(s, axis=-1, keepdims=True)
    p = jnp.exp(s - m)
    l = jnp.sum(p, axis=-1, keepdims=True)
    o = jnp.dot(p.astype(BF16), v_ref[0, 0], preferred_element_type=F32)
    o_ref[0] = (o / l).reshape(g, tq, hd)


def attention(q, kt, v, tq):
    B, H, Lq, hd = q.shape
    KV, Lk = kt.shape[1], kt.shape[3]
    G = H // KV
    return pl.pallas_call(
        _attention_kernel,
        grid=(B, KV, Lq // tq),
        in_specs=[
            pl.BlockSpec((1, G, tq, hd), lambda b, k, i: (b, k, i, 0)),
            pl.BlockSpec((1, 1, hd, Lk), lambda b, k, i: (b, k, 0, 0)),
            pl.BlockSpec((1, 1, Lk, hd), lambda b, k, i: (b, k, 0, 0)),
        ],
        out_specs=pl.BlockSpec((1, G, tq, hd), lambda b, k, i: (b, k, i, 0)),
        out_shape=jax.ShapeDtypeStruct((B, H, Lq, hd), F32),
        compiler_params=_params(("parallel", "parallel", "parallel")),
        name="gqa_attention",
    )(q, kt, v)


def _expert_ffn_kernel(x_ref, wg_ref, wu_ref, wd_ref, gate_ref, o_ref):
    f = pl.program_id(1)
    x = x_ref[0]
    a = jnp.dot(x, wg_ref[0].astype(BF16), preferred_element_type=F32)
    u = jnp.dot(x, wu_ref[0].astype(BF16), preferred_element_type=F32)
    h = (a * jax.nn.sigmoid(a) * u).astype(BF16)
    y = jnp.dot(h, wd_ref[0].astype(BF16), preferred_element_type=F32)

    @pl.when(f == 0)
    def _():
        o_ref[0] = y

    @pl.when(f > 0)
    def _():
        o_ref[0] += y

    @pl.when(f == pl.num_programs(1) - 1)
    def _():
        o_ref[0] *= gate_ref[0]


def expert_ffn(xg, w_gate, w_up, w_down, gate, tf):
    E, R, D = xg.shape
    F = w_gate.shape[2]
    return pl.pallas_call(
        _expert_ffn_kernel,
        grid=(E, F // tf),
        in_specs=[
            pl.BlockSpec((1, R, D), lambda e, f: (e, 0, 0)),
            pl.BlockSpec((1, D, tf), lambda e, f: (e, 0, f)),
            pl.BlockSpec((1, D, tf), lambda e, f: (e, 0, f)),
            pl.BlockSpec((1, tf, D), lambda e, f: (e, f, 0)),
            pl.BlockSpec((1, R, 1), lambda e, f: (e, 0, 0)),
        ],
        out_specs=pl.BlockSpec((1, R, D), lambda e, f: (e, 0, 0)),
        out_shape=jax.ShapeDtypeStruct((E, R, D), F32),
        compiler_params=_params(("parallel", "arbitrary")),
        name="expert_ffn",
    )(xg, w_gate, w_up, w_down, gate)


def _short_conv_kernel(p_ref, w_ref, b_ref, o_ref):
    q = p_ref[0]
    L = q.shape[0]
    row = lax.broadcasted_iota(jnp.int32, q.shape, 0)
    prev = jnp.where(row == 0, 0.0, pltpu.roll(q, 1, 0))
    nxt = jnp.where(row == L - 1, 0.0, pltpu.roll(q, L - 1, 0))
    w = w_ref[...]
    o_ref[0, 0] = prev * w[0:1] + q * w[1:2] + nxt * w[2:3] + b_ref[...]


def short_conv(p, conv_w, conv_b, tc=256):
    B, L, D3 = p.shape
    D = D3 // 3
    nj = D // tc
    return pl.pallas_call(
        _short_conv_kernel,
        grid=(3, B, nj),
        in_specs=[
            pl.BlockSpec((1, L, tc), lambda s, b, j: (b, 0, s * nj + j)),
            pl.BlockSpec((3, tc), lambda s, b, j: (0, s * nj + j)),
            pl.BlockSpec((1, tc), lambda s, b, j: (0, s * nj + j)),
        ],
        out_specs=pl.BlockSpec((1, 1, L, tc), lambda s, b, j: (s, b, 0, j)),
        out_shape=jax.ShapeDtypeStruct((3, B, L, D), F32),
        compiler_params=_params(("parallel", "parallel", "parallel")),
        name="hyena_short_conv",
    )(p, conv_w, conv_b[None])


FFT_GROUP = 128
FFT_PASSES = 3


def _split_hi_lo(m):
    hi = m.astype(BF16)
    return hi, (m - hi.astype(F32)).astype(BF16)


def _dot_split(m_hi, m_lo, x):
    x_hi = x.astype(BF16)
    acc = jnp.dot(m_hi, x_hi, preferred_element_type=F32)
    if FFT_PASSES >= 2:
        x_lo = (x - x_hi.astype(F32)).astype(BF16)
        acc += jnp.dot(m_hi, x_lo, preferred_element_type=F32)
    if FFT_PASSES >= 3:
        acc += jnp.dot(m_lo, x_hi, preferred_element_type=F32)
    return acc


def _cos_sin(num, den):
    ang = (2.0 * math.pi / den) * (num % den).astype(F32)
    return jnp.cos(ang), jnp.sin(ang)


def _outer_tables(N1, n_in):
    k1 = jnp.arange(N1, dtype=jnp.int32)
    c, s = _cos_sin(k1[:, None] * jnp.arange(n_in, dtype=jnp.int32)[None, :], N1)
    fwd = jnp.concatenate([jnp.concatenate([c, s], 1), jnp.concatenate([-s, c], 1)], 0)
    cf, sf = _cos_sin(k1[:, None] * k1[None, :], N1)
    fwd_real = jnp.concatenate([cf, -sf], 0)
    ci, si = c.T, s.T
    inv = jnp.concatenate([jnp.concatenate([ci, -si], 1), jnp.concatenate([si, ci], 1)], 0) / (N1 * FFT_GROUP)
    return fwd, fwd_real, inv


def _group_tables(N1):
    G = FFT_GROUP
    k = jnp.arange(N1, dtype=jnp.int32)[:, None, None] + N1 * jnp.arange(G, dtype=jnp.int32)[None, :, None]
    gr, gs = _cos_sin(k * jnp.arange(G, dtype=jnp.int32)[None, None, :], N1 * G)
    gi = -gs
    fwd = jnp.concatenate([jnp.concatenate([gr, -gi], 2), jnp.concatenate([gi, gr], 2)], 1)
    return fwd, jnp.swapaxes(fwd, 1, 2)


def _outer_fwd_kernel(z_ref, m_hi, m_lo, o_ref):
    nb, w = z_ref.shape[1], z_ref.shape[2]
    x = z_ref[...].reshape(z_ref.shape[0] * nb, w)
    r = _dot_split(m_hi[...], m_lo[...], x)
    o_ref[0] = r.reshape(2, r.shape[0] // 2, w)


def outer_fwd(z, m, parts, w):
    Bz, n_in, C = z.shape
    P = Bz // parts
    N1 = m.shape[0] // 2
    m_hi, m_lo = _split_hi_lo(m)
    return pl.pallas_call(
        _outer_fwd_kernel,
        grid=(P, C // w),
        in_specs=[
            pl.BlockSpec((parts, n_in, w), lambda p, j: (p, 0, j)),
            pl.BlockSpec(m.shape, lambda p, j: (0, 0)),
            pl.BlockSpec(m.shape, lambda p, j: (0, 0)),
        ],
        out_specs=pl.BlockSpec((1, 2, N1, w), lambda p, j: (p, 0, 0, j)),
        out_shape=jax.ShapeDtypeStruct((P, 2, N1, C), F32),
        compiler_params=_params(("parallel", "parallel")),
        name="hyena_dft_outer_fwd",
    )(z, m_hi, m_lo)


def _group_spectrum_kernel(a_ref, g_hi, g_lo, o_ref):
    a = a_ref[0]
    x = _dot_split(g_hi[0], g_lo[0], a.reshape(a.shape[0] * a.shape[1], a.shape[2]))
    o_ref[...] = x.reshape(2, x.shape[0] // 2, x.shape[1])


def group_spectrum(a, g):
    _, parts, rows, D = a.shape
    n_grp, Rm2 = g.shape[0], g.shape[1]
    Rin = rows // n_grp
    g_hi, g_lo = _split_hi_lo(g)
    return pl.pallas_call(
        _group_spectrum_kernel,
        grid=(n_grp,),
        in_specs=[
            pl.BlockSpec((1, parts, Rin, D), lambda k: (0, 0, k, 0)),
            pl.BlockSpec((1,) + g.shape[1:], lambda k: (k, 0, 0)),
            pl.BlockSpec((1,) + g.shape[1:], lambda k: (k, 0, 0)),
        ],
        out_specs=pl.BlockSpec((2, Rm2 // 2, D), lambda k: (0, k, 0)),
        out_shape=jax.ShapeDtypeStruct((2, n_grp * Rm2 // 2, D), F32),
        compiler_params=_params(("parallel",)),
        name="hyena_filter_spectrum",
    )(a, g_hi, g_lo)


def _group_conv_kernel(a_ref, kf_ref, g_hi, g_lo, h_hi, h_lo, o_ref):
    a = a_ref[0]
    x = _dot_split(g_hi[0], g_lo[0], a.reshape(a.shape[0] * a.shape[1], a.shape[2]))
    r = x.shape[0] // 2
    xr, xi = x[:r], x[r:]
    kr, ki = kf_ref[0], kf_ref[1]
    y = jnp.concatenate([xr * kr - xi * ki, xr * ki + xi * kr], axis=0)
    p = _dot_split(h_hi[0], h_lo[0], y)
    o_ref[0] = p.reshape(2, p.shape[0] // 2, p.shape[1])


def group_conv(a, kf, g, h):
    P, _, rows, D = a.shape
    n_grp, Rm2, Rout2 = g.shape[0], g.shape[1], h.shape[1]
    Rin = rows // n_grp
    g_hi, g_lo = _split_hi_lo(g)
    h_hi, h_lo = _split_hi_lo(h)
    gspec = pl.BlockSpec((1,) + g.shape[1:], lambda k, p: (k, 0, 0))
    hspec = pl.BlockSpec((1,) + h.shape[1:], lambda k, p: (k, 0, 0))
    return pl.pallas_call(
        _group_conv_kernel,
        grid=(n_grp, P),
        in_specs=[
            pl.BlockSpec((1, 2, Rin, D), lambda k, p: (p, 0, k, 0)),
            pl.BlockSpec((2, Rm2 // 2, D), lambda k, p: (0, k, 0)),
            gspec, gspec, hspec, hspec,
        ],
        out_specs=pl.BlockSpec((1, 2, Rout2 // 2, D), lambda k, p: (p, 0, k, 0)),
        out_shape=jax.ShapeDtypeStruct((P, 2, n_grp * Rout2 // 2, D), F32),
        compiler_params=_params(("parallel", "parallel")),
        name="hyena_group_conv",
    )(a, kf, g_hi, g_lo, h_hi, h_lo)


def _outer_inv_kernel(p_ref, m_hi, m_lo, gate_ref, u_ref, bias_ref, o_ref):
    pr = p_ref[0]
    y = _dot_split(m_hi[...], m_lo[...], pr.reshape(pr.shape[0] * pr.shape[1], pr.shape[2]))
    y = y.reshape(2, y.shape[0] // 2, y.shape[1])
    o_ref[...] = gate_ref[...] * (y + u_ref[...] * bias_ref[...])


def outer_inv(pm, m, gate, u, bias_w, w):
    P, _, N1, C = pm.shape
    n_out = m.shape[0] // 2
    m_hi, m_lo = _split_hi_lo(m)
    bspec = pl.BlockSpec((2, n_out, w), lambda p, j: (p, 0, j))
    return pl.pallas_call(
        _outer_inv_kernel,
        grid=(P, C // w),
        in_specs=[
            pl.BlockSpec((1, 2, N1, w), lambda p, j: (p, 0, 0, j)),
            pl.BlockSpec(m.shape, lambda p, j: (0, 0)),
            pl.BlockSpec(m.shape, lambda p, j: (0, 0)),
            bspec, bspec,
            pl.BlockSpec((1, w), lambda p, j: (0, 0)),
        ],
        out_specs=bspec,
        out_shape=jax.ShapeDtypeStruct((2 * P, n_out, C), F32),
        compiler_params=_params(("parallel", "parallel")),
        name="hyena_dft_outer_inv",
    )(pm, m_hi, m_lo, gate, u, bias_w)


def _hyena_long_convs(v, x1, x2, k2, f_bias):
    B, L, D = v.shape
    N = 2 * L
    G = FFT_GROUP
    if N // G >= 16:
        N1, Nh = N // G, N // G // 2
        w = 8 * D
        fwd, fwd_real, inv = _outer_tables(N1, Nh)
        g_fwd, g_inv = _group_tables(N1)
        view = lambda t: t.reshape(B, Nh, G * D)
        z = v
        for n, gate in enumerate((x1, x2)):
            kf = group_spectrum(outer_fwd(k2[:, n].reshape(1, N1, G * D), fwd_real, 1, w).reshape(1, 2, N, D), g_fwd)
            a = outer_fwd(view(z), fwd, 2, w).reshape(B // 2, 2, N, D)
            pm = group_conv(a, kf, g_fwd, g_inv).reshape(B // 2, 2, N1, G * D)
            bias_w = jnp.tile(f_bias[n], w // D)[None]
            z = outer_inv(pm, inv, view(gate), view(z), bias_w, w).reshape(B, L, D)
        return z
    kk = jnp.arange(N, dtype=jnp.int32)
    c, s = _cos_sin(kk[:, None] * kk[None, :], N)
    g_real = jnp.concatenate([c, -s], 0)[None]
    ch, sh = c[:, :L], s[:, :L]
    g_fwd = jnp.concatenate([jnp.concatenate([ch, sh], 1), jnp.concatenate([-sh, ch], 1)], 0)[None]
    ci, si = ch.T, sh.T
    g_inv = (jnp.concatenate([jnp.concatenate([ci, -si], 1), jnp.concatenate([si, ci], 1)], 0) / N)[None]
    z = v
    for n, gate in enumerate((x1, x2)):
        kf = group_spectrum(k2[:, n].reshape(1, 1, N, D), g_real)
        y = group_conv(z.reshape(B // 2, 2, L, D), kf, g_fwd, g_inv).reshape(B, L, D)
        z = gate * (y + z * f_bias[n])
    return z


def _adaln(cond, w_mod, b_mod):
    m = jax.nn.silu(cond) @ w_mod + b_mod
    return jnp.split(m, 6, axis=-1)


def _rms_norm(x, g):
    return x * lax.rsqrt(jnp.mean(jnp.square(x), axis=-1, keepdims=True) + RMS_EPS) * g


def _hyena_filter_taps(L, f_w1, f_b1, f_w2, f_b2, f_w3, f_freq):
    t = jnp.linspace(0.0, 1.0, L, dtype=F32)[:, None]
    w = (2.0 * math.pi / L) * jnp.arange(L, dtype=F32)[:, None]
    f = jnp.linspace(1e-4, HY_BANDS - 1, HY_BANDS, dtype=F32)[None, :]
    z = jnp.concatenate([t, jnp.cos(f * w), -jnp.sin(f * w)], axis=-1)
    h = jnp.sin(f_freq[0] * (z @ f_w1 + f_b1))
    h = jnp.sin(f_freq[1] * (h @ f_w2 + f_b2))
    h = (h @ f_w3).reshape(L, HY_ORDER, 2, D_MODEL)
    deltas = jnp.abs(jnp.linspace(HY_MIN_DECAY, HY_MAX_DECAY, D_MODEL, dtype=F32))
    h = h * jnp.exp(-t * deltas)[:, None, None, :]
    fwd, bwd = h[:, :, 0], h[:, :, 1]
    k2 = jnp.concatenate([fwd, jnp.zeros((1, HY_ORDER, D_MODEL), F32), bwd[:0:-1]], axis=0)
    return k2 / (jnp.sum(jnp.abs(k2), axis=0, keepdims=True) + RMS_EPS)


def _hyena_core(p, conv_w, conv_b, f_w1, f_b1, f_w2, f_b2, f_w3, f_freq, f_bias):
    L = p.shape[1]
    vx = short_conv(p, conv_w, conv_b)
    k2 = _hyena_filter_taps(L, f_w1, f_b1, f_w2, f_b2, f_w3, f_freq)
    return _hyena_long_convs(vx[0], vx[1], vx[2], k2, f_bias)


def _axial_rope_tables(L):
    rows = L // GRID_W
    row = jnp.broadcast_to(jnp.arange(rows, dtype=F32)[:, None], (rows, GRID_W)).reshape(L)
    col = jnp.broadcast_to(jnp.arange(GRID_W, dtype=F32)[None, :], (rows, GRID_W)).reshape(L)
    n_freq = HEAD_DIM // 4
    inv_freq = ROPE_THETA ** (-jnp.arange(n_freq, dtype=F32) / n_freq)
    ang = jnp.concatenate([row[:, None] * inv_freq, col[:, None] * inv_freq], axis=-1)
    return jnp.cos(ang), jnp.sin(ang)


def _apply_rope(x, cos, sin):
    shape = (1, x.shape[1]) + (1,) * (x.ndim - 3) + (HEAD_DIM // 2,)
    cos, sin = cos.reshape(shape), sin.reshape(shape)
    x1, x2 = x[..., :HEAD_DIM // 2], x[..., HEAD_DIM // 2:]
    return jnp.concatenate([x1 * cos - x2 * sin, x1 * sin + x2 * cos], axis=-1)


def _expert_choice_ffn(x, shift, scale, w_router, w_gate, w_up, w_down):
    B, n, D = x.shape
    cap = CAPACITY_FACTOR * n // N_EXPERTS
    h = x * (1.0 + scale) + shift
    aff = jax.nn.softmax(h @ w_router, axis=-1)
    gate, idx = lax.top_k(jnp.swapaxes(aff, 1, 2), cap)
    bidx = jnp.arange(B)[:, None, None]
    xg = h.astype(BF16)[bidx, idx]
    xg = jnp.swapaxes(xg, 0, 1).reshape(N_EXPERTS, B * cap, D)
    gt = jnp.swapaxes(gate, 0, 1).reshape(N_EXPERTS, B * cap, 1)
    y = expert_ffn(xg, w_gate, w_up, w_down, gt, tf=512)
    y = jnp.swapaxes(y.reshape(N_EXPERTS, B, cap, D), 0, 1)
    return jnp.zeros_like(x).at[bidx, idx].add(y)


def kernel(x, c, ctx, c_ctx, mod_w, mod_b, ln_g, ln_b, hy_w_in, hy_conv_w, hy_conv_b, hy_f_w1, hy_f_b1, hy_f_w2, hy_f_b2, hy_f_w3, hy_f_freq, hy_f_bias, hy_w_out, at_w_qkv, at_q_gain, at_k_gain, at_w_out, cm_w_in, cm_ln_g, cm_ln_b, cm_w_s, cm_b_s, cm_w_out, moe_router, moe_w_gate, moe_w_up, moe_w_down):
    B, L, D = x.shape
    Lc = ctx.shape[1]
    x_lat, x_ctx = x, ctx
    for i in range(DEPTH):
        kind, slot = LAYER_KINDS[i], KIND_SLOT[i]
        ctx_read = i <= LAST_CTX_LAYER
        ctx_update = i < LAST_CTX_LAYER
        sh1, sc1, g1, sh2, sc2, g2 = _adaln(c[:, None, :], mod_w[i], mod_b[i])
        if ctx_read:
            cmod = [jnp.broadcast_to(m[None, None, :], (B, 1, D)) for m in _adaln(c_ctx, mod_w[i], mod_b[i])]
        lg0, lb0, lg1, lb1 = ln_g[i, 0][None], ln_b[i, 0][None], ln_g[i, 1][None], ln_b[i, 1][None]

        if kind == 0:
            w_in = hy_w_in[slot].astype(BF16)
            w_out = hy_w_out[slot].astype(BF16)
            hy = (hy_conv_w[slot], hy_conv_b[slot], hy_f_w1[slot], hy_f_b1[slot], hy_f_w2[slot], hy_f_b2[slot],
                  hy_f_w3[slot], hy_f_freq[slot], hy_f_bias[slot])
            z = _hyena_core(mod_matmul(x_lat, sh1, sc1, w_in), *hy)
            x_lat = matmul_postnorm(z, w_out, x_lat, g1, lg0, lb0)
            if ctx_update:
                zc = _hyena_core(mod_matmul(x_ctx, cmod[0], cmod[1], w_in), *hy)
                x_ctx = matmul_postnorm(zc, w_out, x_ctx, cmod[2], lg0, lb0)
        elif kind == 1:
            w_qkv = at_w_qkv[slot].astype(BF16)
            w_out = at_w_out[slot].astype(BF16)
            cos, sin = _axial_rope_tables(L)
            qkv = mod_matmul(x_lat, sh1, sc1, w_qkv)
            q = _rms_norm(qkv[..., :Q_COLS].reshape(B, L, N_Q_HEADS, HEAD_DIM), at_q_gain[slot])
            q = _apply_rope(q, cos, sin) * ATTN_SCALE
            k, v = jnp.split(qkv[..., Q_COLS:], 2, axis=-1)
            k = _apply_rope(_rms_norm(k.reshape(B, L, N_KV_HEADS, HEAD_DIM), at_k_gain[slot]), cos, sin)
            v = v.reshape(B, L, N_KV_HEADS, HEAD_DIM)
            kvc = mod_matmul(x_ctx, cmod[0], cmod[1], w_qkv[:, Q_COLS:])
            ck, cv = jnp.split(kvc, 2, axis=-1)
            ck = _rms_norm(ck.reshape(B, Lc, N_KV_HEADS, HEAD_DIM), at_k_gain[slot])
            cv = cv.reshape(B, Lc, N_KV_HEADS, HEAD_DIM)
            keys = jnp.concatenate([ck, k], axis=1).astype(BF16)
            vals = jnp.concatenate([cv, v], axis=1).astype(BF16)
            o = attention(jnp.transpose(q.astype(BF16), (0, 2, 1, 3)),
                          jnp.transpose(keys, (0, 2, 3, 1)),
                          jnp.transpose(vals, (0, 2, 1, 3)), tq=128)
            o = jnp.transpose(o, (0, 2, 1, 3)).reshape(B, L, Q_COLS)
            x_lat = matmul_postnorm(o, w_out, x_lat, g1, lg0, lb0)
        else:
            w_in = cm_w_in[slot].astype(BF16)
            w_out = cm_w_out[slot].astype(BF16)
            zz = jax.nn.gelu(mod_matmul(x_lat, sh1, sc1, w_in), approximate=False)
            u, v = jnp.split(zz, 2, axis=-1)
            mu = jnp.mean(v, axis=-1, keepdims=True)
            var = jnp.mean(jnp.square(v - mu), axis=-1, keepdims=True)
            v = (v - mu) * lax.rsqrt(var + LN_EPS) * cm_ln_g[slot] + cm_ln_b[slot]
            v = v.reshape(B, L // CHUNK, CHUNK, GMLP_GROUPS, GMLP_GROUP_W)
            v = jnp.einsum('gts,bnsgc->bntgc', cm_w_s[slot], v) + cm_b_s[slot].T[None, None, :, :, None]
            x_lat = matmul_postnorm(u * v.reshape(B, L, GMLP_HALF), w_out, x_lat, g1, lg0, lb0)

        moe = (moe_router[i], moe_w_gate[i], moe_w_up[i], moe_w_down[i])
        x_lat = postnorm(_expert_choice_ffn(x_lat, sh2, sc2, *moe), x_lat, g2, lg1, lb1)
        if ctx_update:
            x_ctx = postnorm(_expert_choice_ffn(x_ctx, cmod[3], cmod[4], *moe), x_ctx, cmod[5], lg1, lb1)
    return x_lat
```

```python
import math
from functools import partial

import jax
import jax.numpy as jnp
from jax import lax
from jax.experimental import pallas as pl
from jax.experimental.pallas import tpu as pltpu

F32 = jnp.float32
BF16 = jnp.bfloat16

D_MODEL = 1024
DEPTH = 4
GRID_W = 64
N_MIXERS = 3
LAYER_KINDS = tuple(i % N_MIXERS for i in range(DEPTH))
KIND_SLOT = tuple(LAYER_KINDS[:i].count(LAYER_KINDS[i]) for i in range(DEPTH))
LAST_CTX_LAYER = max([i for i, k in enumerate(LAYER_KINDS) if k == 1], default=-1)

HY_ORDER = 2
HY_BANDS = 16
HY_DECAY_TARGET = 1e-2
HY_FAST_DECAY_PCT = 0.3
HY_SLOW_DECAY_PCT = 1.5
HY_MIN_DECAY = math.log(HY_DECAY_TARGET) / HY_SLOW_DECAY_PCT
HY_MAX_DECAY = math.log(HY_DECAY_TARGET) / HY_FAST_DECAY_PCT

HEAD_DIM = 64
N_Q_HEADS = D_MODEL // HEAD_DIM
N_KV_HEADS = 4
GQA_GROUP = N_Q_HEADS // N_KV_HEADS
ROPE_THETA = 10000.0
ATTN_SCALE = HEAD_DIM ** -0.5
Q_COLS = N_Q_HEADS * HEAD_DIM

CHUNK = 128
GMLP_HALF = 2 * D_MODEL
GMLP_GROUPS = 16
GMLP_GROUP_W = GMLP_HALF // GMLP_GROUPS

N_EXPERTS = 16
CAPACITY_FACTOR = 2

RMS_EPS = 1e-6
LN_EPS = 1e-5
DEEPNORM_ALPHA = (2 * DEPTH) ** 0.25

VMEM_LIMIT_BYTES = 56 * 1024 * 1024
OUT_TILE_BYTES = 4 * 1024 * 1024


def _row_tile(rows, n_cols):
    tm = 512
    while tm > 8 and (tm * n_cols * 4 > OUT_TILE_BYTES or rows % tm):
        tm //= 2
    return tm


def _params(sem):
    return pltpu.CompilerParams(dimension_semantics=sem, vmem_limit_bytes=VMEM_LIMIT_BYTES)


def _mod_matmul_kernel(x_ref, sh_ref, sc_ref, w_ref, o_ref):
    h = x_ref[0] * (1.0 + sc_ref[0]) + sh_ref[0]
    o_ref[0] = jnp.dot(h.astype(BF16), w_ref[...], preferred_element_type=F32)


def mod_matmul(x, shift, scale, w):
    B, L, K = x.shape
    N = w.shape[1]
    tm = _row_tile(L, N)
    return pl.pallas_call(
        _mod_matmul_kernel,
        grid=(B, L // tm),
        in_specs=[
            pl.BlockSpec((1, tm, K), lambda b, i: (b, i, 0)),
            pl.BlockSpec((1, 1, K), lambda b, i: (b, 0, 0)),
            pl.BlockSpec((1, 1, K), lambda b, i: (b, 0, 0)),
            pl.BlockSpec((K, N), lambda b, i: (0, 0)),
        ],
        out_specs=pl.BlockSpec((1, tm, N), lambda b, i: (b, i, 0)),
        out_shape=jax.ShapeDtypeStruct((B, L, N), F32),
        compiler_params=_params(("parallel", "parallel")),
        name="mod_matmul",
    )(x, shift, scale, w)


def _layer_norm_rows(t, g, b):
    mu = jnp.mean(t, axis=-1, keepdims=True)
    d = t - mu
    var = jnp.mean(d * d, axis=-1, keepdims=True)
    return d * lax.rsqrt(var + LN_EPS) * g + b


def _matmul_postnorm_kernel(z_ref, w_ref, x_ref, gate_ref, g_ref, b_ref, o_ref):
    y = jnp.dot(z_ref[0].astype(BF16), w_ref[...], preferred_element_type=F32)
    t = DEEPNORM_ALPHA * x_ref[0] + gate_ref[0] * y
    o_ref[0] = _layer_norm_rows(t, g_ref[...], b_ref[...])


def matmul_postnorm(z, w, x, gate, g, b):
    B, L, K = z.shape
    D = w.shape[1]
    tm = _row_tile(L, max(K, D))
    return pl.pallas_call(
        _matmul_postnorm_kernel,
        grid=(B, L // tm),
        in_specs=[
            pl.BlockSpec((1, tm, K), lambda bi, i: (bi, i, 0)),
            pl.BlockSpec((K, D), lambda bi, i: (0, 0)),
            pl.BlockSpec((1, tm, D), lambda bi, i: (bi, i, 0)),
            pl.BlockSpec((1, 1, D), lambda bi, i: (bi, 0, 0)),
            pl.BlockSpec((1, D), lambda bi, i: (0, 0)),
            pl.BlockSpec((1, D), lambda bi, i: (0, 0)),
        ],
        out_specs=pl.BlockSpec((1, tm, D), lambda bi, i: (bi, i, 0)),
        out_shape=jax.ShapeDtypeStruct((B, L, D), F32),
        compiler_params=_params(("parallel", "parallel")),
        name="matmul_postnorm",
    )(z, w, x, gate, g, b)


def _postnorm_kernel(y_ref, x_ref, gate_ref, g_ref, b_ref, o_ref):
    t = DEEPNORM_ALPHA * x_ref[0] + gate_ref[0] * y_ref[0]
    o_ref[0] = _layer_norm_rows(t, g_ref[...], b_ref[...])


def postnorm(y, x, gate, g, b):
    B, L, D = x.shape
    tm = _row_tile(L, D)
    return pl.pallas_call(
        _postnorm_kernel,
        grid=(B, L // tm),
        in_specs=[
            pl.BlockSpec((1, tm, D), lambda bi, i: (bi, i, 0)),
            pl.BlockSpec((1, tm, D), lambda bi, i: (bi, i, 0)),
            pl.BlockSpec((1, 1, D), lambda bi, i: (bi, 0, 0)),
            pl.BlockSpec((1, D), lambda bi, i: (0, 0)),
            pl.BlockSpec((1, D), lambda bi, i: (0, 0)),
        ],
        out_specs=pl.BlockSpec((1, tm, D), lambda bi, i: (bi, i, 0)),
        out_shape=jax.ShapeDtypeStruct((B, L, D), F32),
        compiler_params=_params(("parallel", "parallel")),
        name="postnorm",
    )(y, x, gate, g, b)


def _head_rms_norm(x, ones_bd, gain):
    sq = x * x
    sq_hi = sq.astype(BF16)
    sq_lo = (sq - sq_hi.astype(F32)).astype(BF16)
    ss = jnp.dot(sq_hi, ones_bd, preferred_element_type=F32) + jnp.dot(sq_lo, ones_bd, preferred_element_type=F32)
    return x * lax.rsqrt(ss * (1.0 / HEAD_DIM) + RMS_EPS) * gain


def _rope(x, cos_t, sin_t):
    w = x.shape[1]
    lane = lax.broadcasted_iota(jnp.int32, x.shape, 1)
    half = HEAD_DIM // 2
    partner = jnp.where((lane & (HEAD_DIM - 1)) < half, pltpu.roll(x, w - half, 1), pltpu.roll(x, half, 1))
    reps = w // cos_t.shape[1]
    return x * jnp.tile(cos_t, (1, reps)) + partner * jnp.tile(sin_t, (1, reps))


def _qkv_prep_kernel(*refs, n_q, rope):
    refs = list(refs)
    x_ref = refs.pop(0)
    cos_ref, sin_ref = (refs.pop(0), refs.pop(0)) if rope else (None, None)
    if n_q:
        qg_ref, qbd_ref = refs.pop(0), refs.pop(0)
    kg_ref, kbd_ref = refs.pop(0), refs.pop(0)
    if n_q:
        q_out = refs.pop(0)
    k_out, v_out = refs
    x = x_ref[0]
    n_kv = k_out.shape[1]
    qc, kc = n_q * HEAD_DIM, n_kv * HEAD_DIM
    if n_q:
        q = _head_rms_norm(x[:, :qc], qbd_ref[...], qg_ref[...])
        if rope:
            q = _rope(q, cos_ref[...], sin_ref[...])
        q = q * ATTN_SCALE
        for h in range(n_q):
            q_out[0, h] = q[:, h * HEAD_DIM:(h + 1) * HEAD_DIM].astype(BF16)
    k = _head_rms_norm(x[:, qc:qc + kc], kbd_ref[...], kg_ref[...])
    if rope:
        k = _rope(k, cos_ref[...], sin_ref[...])
    v = x[:, qc + kc:qc + 2 * kc]
    for h in range(n_kv):
        k_out[0, h] = k[:, h * HEAD_DIM:(h + 1) * HEAD_DIM].astype(BF16)
        v_out[0, h] = v[:, h * HEAD_DIM:(h + 1) * HEAD_DIM].astype(BF16)


def qkv_prep(x, q_gain, k_gain, n_q, n_kv, rope_tables, tm=256):
    B, L, C = x.shape
    hd = HEAD_DIM
    tm = min(tm, L)

    def ones_bd(n):
        r = jnp.arange(n * hd) // hd
        return (r[:, None] == r[None, :]).astype(BF16)

    const = lambda shape: pl.BlockSpec(shape, lambda b, i: (0,) * len(shape))
    args, specs = [x], [pl.BlockSpec((1, tm, C), lambda b, i: (b, i, 0))]
    if rope_tables is not None:
        args += list(rope_tables)
        specs += [pl.BlockSpec((tm, 128), lambda b, i: (i, 0))] * 2
    if n_q:
        args += [jnp.tile(q_gain, n_q)[None], ones_bd(n_q)]
        specs += [const((1, n_q * hd)), const((n_q * hd, n_q * hd))]
    args += [jnp.tile(k_gain, n_kv)[None], ones_bd(n_kv)]
    specs += [const((1, n_kv * hd)), const((n_kv * hd, n_kv * hd))]
    head_spec = lambda n: pl.BlockSpec((1, n, tm, hd), lambda b, i: (b, 0, i, 0))
    head_shape = lambda n: jax.ShapeDtypeStruct((B, n, L, hd), BF16)
    out_specs = ([head_spec(n_q)] if n_q else []) + [head_spec(n_kv), head_spec(n_kv)]
    out_shape = ([head_shape(n_q)] if n_q else []) + [head_shape(n_kv), head_shape(n_kv)]
    return pl.pallas_call(
        partial(_qkv_prep_kernel, n_q=n_q, rope=rope_tables is not None),
        grid=(B, L // tm),
        in_specs=specs,
        out_specs=out_specs,
        out_shape=out_shape,
        compiler_params=_params(("parallel", "parallel")),
        name="qkv_prep",
    )(*args)


def _nt_dot(a, b):
    return lax.dot_general(a, b, (((1,), (1,)), ((), ())), preferred_element_type=F32)


def _attention_kernel(q_ref, ck_ref, cv_ref, k_ref, v_ref, o_ref):
    g, tq, hd = q_ref.shape[1], q_ref.shape[2], q_ref.shape[3]
    q = q_ref[0].reshape(g * tq, hd)
    s_c = _nt_dot(q, ck_ref[0, 0])
    s_l = _nt_dot(q, k_ref[0, 0])
    m = jnp.maximum(jnp.max(s_c, axis=-1, keepdims=True), jnp.max(s_l, axis=-1, keepdims=True))
    p_c = jnp.exp(s_c - m)
    p_l = jnp.exp(s_l - m)
    l = jnp.sum(p_c, axis=-1, keepdims=True) + jnp.sum(p_l, axis=-1, keepdims=True)
    o = (jnp.dot(p_c.astype(BF16), cv_ref[0, 0], preferred_element_type=F32)
         + jnp.dot(p_l.astype(BF16), v_ref[0, 0], preferred_element_type=F32)) / l
    o_ref[0] = o.astype(BF16).reshape(g, tq, hd)


def attention(q, ck, cv, k, v, tq=128):
    B, H, L, hd = q.shape
    KV, Lc = ck.shape[1], ck.shape[2]
    G = H // KV
    kv_spec = lambda n: pl.BlockSpec((1, 1, n, hd), lambda b, kv, i: (b, kv, 0, 0))
    q_spec = pl.BlockSpec((1, G, tq, hd), lambda b, kv, i: (b, kv, i, 0))
    return pl.pallas_call(
        _attention_kernel,
        grid=(B, KV, L // tq),
        in_specs=[q_spec, kv_spec(Lc), kv_spec(Lc), kv_spec(L), kv_spec(L)],
        out_specs=q_spec,
        out_shape=jax.ShapeDtypeStruct((B, H, L, hd), BF16),
        compiler_params=_params(("parallel", "parallel", "parallel")),
        name="gqa_attention",
    )(q, ck, cv, k, v)


def _heads_matmul_postnorm_kernel(o_ref, w_ref, x_ref, gate_ref, g_ref, b_ref, out_ref):
    y = jnp.dot(o_ref[0, 0], w_ref[0], preferred_element_type=F32)
    for h in range(1, o_ref.shape[1]):
        y += jnp.dot(o_ref[0, h], w_ref[h], preferred_element_type=F32)
    t = DEEPNORM_ALPHA * x_ref[0] + gate_ref[0] * y
    out_ref[0] = _layer_norm_rows(t, g_ref[...], b_ref[...])


def heads_matmul_postnorm(o, w, x, gate, g, b, tm=512):
    B, H, L, hd = o.shape
    D = w.shape[2]
    return pl.pallas_call(
        _heads_matmul_postnorm_kernel,
        grid=(B, L // tm),
        in_specs=[
            pl.BlockSpec((1, H, tm, hd), lambda bi, i: (bi, 0, i, 0)),
            pl.BlockSpec((H, hd, D), lambda bi, i: (0, 0, 0)),
            pl.BlockSpec((1, tm, D), lambda bi, i: (bi, i, 0)),
            pl.BlockSpec((1, 1, D), lambda bi, i: (bi, 0, 0)),
            pl.BlockSpec((1, D), lambda bi, i: (0, 0)),
            pl.BlockSpec((1, D), lambda bi, i: (0, 0)),
        ],
        out_specs=pl.BlockSpec((1, tm, D), lambda bi, i: (bi, i, 0)),
        out_shape=jax.ShapeDtypeStruct((B, L, D), F32),
        compiler_params=_params(("parallel", "parallel")),
        name="heads_matmul_postnorm",
    )(o, w, x, gate, g, b)


def _gather_rows_kernel(idx_ref, h_ref, o_ref, *, unroll):
    cap = o_ref.shape[0]
    base = (pl.program_id(0) * pl.num_programs(1) + pl.program_id(1)) * cap

    def body(s, carry):
        for u in range(unroll):
            r = s * unroll + u
            o_ref[r, 0] = h_ref[idx_ref[base + r], 0]
        return carry

    lax.fori_loop(0, cap // unroll, body, 0)


def gather_rows(h, idx):
    B, n, D = h.shape
    _, E, cap = idx.shape
    out = pl.pallas_call(
        partial(_gather_rows_kernel, unroll=8),
        grid_spec=pltpu.PrefetchScalarGridSpec(
            num_scalar_prefetch=1,
            grid=(B, E),
            in_specs=[pl.BlockSpec((None, n, 1, D), lambda b, e, idx: (b, 0, 0, 0))],
            out_specs=pl.BlockSpec((None, None, cap, 1, D), lambda b, e, idx: (e, b, 0, 0, 0)),
        ),
        out_shape=jax.ShapeDtypeStruct((E, B, cap, 1, D), F32),
        compiler_params=_params(("parallel", "arbitrary")),
        name="moe_gather_rows",
    )(idx.reshape(-1), h.reshape(B, n, 1, D))
    return out.reshape(E, B, cap, D)


def _scatter_add_kernel(idx_ref, y_ref, o_ref, *, unroll):
    cap = y_ref.shape[0]
    base = (pl.program_id(0) * pl.num_programs(1) + pl.program_id(1)) * cap

    @pl.when(pl.program_id(1) == 0)
    def _():
        o_ref[...] = jnp.zeros(o_ref.shape, o_ref.dtype)

    def body(s, carry):
        rows = [idx_ref[base + s * unroll + u] for u in range(unroll)]
        sums = [o_ref[rows[u], 0] + y_ref[s * unroll + u, 0] for u in range(unroll)]
        for u in range(unroll):
            o_ref[rows[u], 0] = sums[u]
        return carry

    lax.fori_loop(0, cap // unroll, body, 0)


def scatter_add_rows(y, idx, n):
    E, B, cap, D = y.shape
    out = pl.pallas_call(
        partial(_scatter_add_kernel, unroll=8),
        grid_spec=pltpu.PrefetchScalarGridSpec(
            num_scalar_prefetch=1,
            grid=(B, E),
            in_specs=[pl.BlockSpec((None, None, cap, 1, D), lambda b, e, idx: (e, b, 0, 0, 0))],
            out_specs=pl.BlockSpec((None, n, 1, D), lambda b, e, idx: (b, 0, 0, 0)),
        ),
        out_shape=jax.ShapeDtypeStruct((B, n, 1, D), F32),
        compiler_params=_params(("parallel", "arbitrary")),
        name="moe_scatter_add_rows",
    )(idx.reshape(-1), y.reshape(E, B, cap, 1, D))
    return out.reshape(B, n, D)


def _expert_ffn_kernel(x_ref, wg_ref, wu_ref, wd_ref, gate_ref, o_ref):
    f = pl.program_id(1)
    x = x_ref[0]
    a = jnp.dot(x, wg_ref[0].astype(BF16), preferred_element_type=F32)
    u = jnp.dot(x, wu_ref[0].astype(BF16), preferred_element_type=F32)
    h = (a * jax.nn.sigmoid(a) * u).astype(BF16)
    y = jnp.dot(h, wd_ref[0].astype(BF16), preferred_element_type=F32)

    @pl.when(f == 0)
    def _():
        o_ref[0] = y

    @pl.when(f > 0)
    def _():
        o_ref[0] += y

    @pl.when(f == pl.num_programs(1) - 1)
    def _():
        o_ref[0] *= gate_ref[0]


def expert_ffn(xg, w_gate, w_up, w_down, gate, tf):
    E, R, D = xg.shape
    F = w_gate.shape[2]
    return pl.pallas_call(
        _expert_ffn_kernel,
        grid=(E, F // tf),
        in_specs=[
            pl.BlockSpec((1, R, D), lambda e, f: (e, 0, 0)),
            pl.BlockSpec((1, D, tf), lambda e, f: (e, 0, f)),
            pl.BlockSpec((1, D, tf), lambda e, f: (e, 0, f)),
            pl.BlockSpec((1, tf, D), lambda e, f: (e, f, 0)),
            pl.BlockSpec((1, R, 1), lambda e, f: (e, 0, 0)),
        ],
        out_specs=pl.BlockSpec((1, R, D), lambda e, f: (e, 0, 0)),
        out_shape=jax.ShapeDtypeStruct((E, R, D), F32),
        compiler_params=_params(("parallel", "arbitrary")),
        name="expert_ffn",
    )(xg, w_gate, w_up, w_down, gate)


def _short_conv_kernel(p_ref, w_ref, b_ref, o_ref):
    q = p_ref[0]
    L = q.shape[0]
    row = lax.broadcasted_iota(jnp.int32, q.shape, 0)
    prev = jnp.where(row == 0, 0.0, pltpu.roll(q, 1, 0))
    nxt = jnp.where(row == L - 1, 0.0, pltpu.roll(q, L - 1, 0))
    w = w_ref[...]
    o_ref[0, 0] = prev * w[0:1] + q * w[1:2] + nxt * w[2:3] + b_ref[...]


def short_conv(p, conv_w, conv_b, tc=256):
    B, L, D3 = p.shape
    D = D3 // 3
    nj = D // tc
    return pl.pallas_call(
        _short_conv_kernel,
        grid=(3, B, nj),
        in_specs=[
            pl.BlockSpec((1, L, tc), lambda s, b, j: (b, 0, s * nj + j)),
            pl.BlockSpec((3, tc), lambda s, b, j: (0, s * nj + j)),
            pl.BlockSpec((1, tc), lambda s, b, j: (0, s * nj + j)),
        ],
        out_specs=pl.BlockSpec((1, 1, L, tc), lambda s, b, j: (s, b, 0, j)),
        out_shape=jax.ShapeDtypeStruct((3, B, L, D), F32),
        compiler_params=_params(("parallel", "parallel", "parallel")),
        name="hyena_short_conv",
    )(p, conv_w, conv_b[None])


FFT_GROUP = 128
FFT_DATA_PASSES = 1
FFT_FILTER_PASSES = 3


def _split_hi_lo(m):
    hi = m.astype(BF16)
    return hi, (m - hi.astype(F32)).astype(BF16)


def _dot_split(m_hi, m_lo, x, passes):
    x_hi = x.astype(BF16)
    acc = jnp.dot(m_hi, x_hi, preferred_element_type=F32)
    if passes >= 2:
        x_lo = (x - x_hi.astype(F32)).astype(BF16)
        acc += jnp.dot(m_hi, x_lo, preferred_element_type=F32)
    if passes >= 3:
        acc += jnp.dot(m_lo, x_hi, preferred_element_type=F32)
    return acc


def _cos_sin(num, den):
    ang = (2.0 * math.pi / den) * (num % den).astype(F32)
    return jnp.cos(ang), jnp.sin(ang)


def _outer_tables(N1, n_in):
    k1 = jnp.arange(N1, dtype=jnp.int32)
    c, s = _cos_sin(k1[:, None] * jnp.arange(n_in, dtype=jnp.int32)[None, :], N1)
    fwd = jnp.concatenate([jnp.concatenate([c, s], 1), jnp.concatenate([-s, c], 1)], 0)
    cf, sf = _cos_sin(k1[:, None] * k1[None, :], N1)
    fwd_real = jnp.concatenate([cf, -sf], 0)
    ci, si = c.T, s.T
    inv = jnp.concatenate([jnp.concatenate([ci, -si], 1), jnp.concatenate([si, ci], 1)], 0) / (N1 * FFT_GROUP)
    return fwd, fwd_real, inv


def _group_tables(N1):
    G = FFT_GROUP
    k = jnp.arange(N1, dtype=jnp.int32)[:, None, None] + N1 * jnp.arange(G, dtype=jnp.int32)[None, :, None]
    gr, gs = _cos_sin(k * jnp.arange(G, dtype=jnp.int32)[None, None, :], N1 * G)
    gi = -gs
    fwd = jnp.concatenate([jnp.concatenate([gr, -gi], 2), jnp.concatenate([gi, gr], 2)], 1)
    return fwd, jnp.swapaxes(fwd, 1, 2)


def _outer_fwd_kernel(z_ref, m_hi, m_lo, o_ref, *, passes):
    nb, w = z_ref.shape[1], z_ref.shape[2]
    x = z_ref[...].reshape(z_ref.shape[0] * nb, w)
    r = _dot_split(m_hi[...], m_lo[...], x, passes)
    o_ref[0] = r.reshape(2, r.shape[0] // 2, w)


def outer_fwd(z, m, parts, w):
    Bz, n_in, C = z.shape
    P = Bz // parts
    N1 = m.shape[0] // 2
    m_hi, m_lo = _split_hi_lo(m)
    return pl.pallas_call(
        partial(_outer_fwd_kernel, passes=FFT_DATA_PASSES if parts == 2 else FFT_FILTER_PASSES),
        grid=(P, C // w),
        in_specs=[
            pl.BlockSpec((parts, n_in, w), lambda p, j: (p, 0, j)),
            pl.BlockSpec(m.shape, lambda p, j: (0, 0)),
            pl.BlockSpec(m.shape, lambda p, j: (0, 0)),
        ],
        out_specs=pl.BlockSpec((1, 2, N1, w), lambda p, j: (p, 0, 0, j)),
        out_shape=jax.ShapeDtypeStruct((P, 2, N1, C), F32),
        compiler_params=_params(("parallel", "parallel")),
        name="hyena_dft_outer_fwd",
    )(z, m_hi, m_lo)


def _group_spectrum_kernel(a_ref, g_hi, g_lo, o_ref):
    a = a_ref[0]
    x = _dot_split(g_hi[0], g_lo[0], a.reshape(a.shape[0] * a.shape[1], a.shape[2]), FFT_FILTER_PASSES)
    o_ref[...] = x.reshape(2, x.shape[0] // 2, x.shape[1])


def group_spectrum(a, g):
    _, parts, rows, D = a.shape
    n_grp, Rm2 = g.shape[0], g.shape[1]
    Rin = rows // n_grp
    g_hi, g_lo = _split_hi_lo(g)
    return pl.pallas_call(
        _group_spectrum_kernel,
        grid=(n_grp,),
        in_specs=[
            pl.BlockSpec((1, parts, Rin, D), lambda k: (0, 0, k, 0)),
            pl.BlockSpec((1,) + g.shape[1:], lambda k: (k, 0, 0)),
            pl.BlockSpec((1,) + g.shape[1:], lambda k: (k, 0, 0)),
        ],
        out_specs=pl.BlockSpec((2, Rm2 // 2, D), lambda k: (0, k, 0)),
        out_shape=jax.ShapeDtypeStruct((2, n_grp * Rm2 // 2, D), F32),
        compiler_params=_params(("parallel",)),
        name="hyena_filter_spectrum",
    )(a, g_hi, g_lo)


def _group_conv_kernel(a_ref, kf_ref, g_hi, g_lo, h_hi, h_lo, o_ref):
    a = a_ref[0]
    x = _dot_split(g_hi[0], g_lo[0], a.reshape(a.shape[0] * a.shape[1], a.shape[2]), FFT_DATA_PASSES)
    r = x.shape[0] // 2
    xr, xi = x[:r], x[r:]
    kr, ki = kf_ref[0], kf_ref[1]
    y = jnp.concatenate([xr * kr - xi * ki, xr * ki + xi * kr], axis=0)
    p = _dot_split(h_hi[0], h_lo[0], y, FFT_DATA_PASSES)
    o_ref[0] = p.reshape(2, p.shape[0] // 2, p.shape[1])


def group_conv(a, kf, g, h):
    P, _, rows, D = a.shape
    n_grp, Rm2, Rout2 = g.shape[0], g.shape[1], h.shape[1]
    Rin = rows // n_grp
    g_hi, g_lo = _split_hi_lo(g)
    h_hi, h_lo = _split_hi_lo(h)
    gspec = pl.BlockSpec((1,) + g.shape[1:], lambda k, p: (k, 0, 0))
    hspec = pl.BlockSpec((1,) + h.shape[1:], lambda k, p: (k, 0, 0))
    return pl.pallas_call(
        _group_conv_kernel,
        grid=(n_grp, P),
        in_specs=[
            pl.BlockSpec((1, 2, Rin, D), lambda k, p: (p, 0, k, 0)),
            pl.BlockSpec((2, Rm2 // 2, D), lambda k, p: (0, k, 0)),
            gspec, gspec, hspec, hspec,
        ],
        out_specs=pl.BlockSpec((1, 2, Rout2 // 2, D), lambda k, p: (p, 0, k, 0)),
        out_shape=jax.ShapeDtypeStruct((P, 2, n_grp * Rout2 // 2, D), F32),
        compiler_params=_params(("parallel", "parallel")),
        name="hyena_group_conv",
    )(a, kf, g_hi, g_lo, h_hi, h_lo)


def _outer_inv_kernel(p_ref, m_hi, m_lo, gate_ref, u_ref, bias_ref, o_ref):
    pr = p_ref[0]
    y = _dot_split(m_hi[...], m_lo[...], pr.reshape(pr.shape[0] * pr.shape[1], pr.shape[2]), FFT_DATA_PASSES)
    y = y.reshape(2, y.shape[0] // 2, y.shape[1])
    o_ref[...] = gate_ref[...] * (y + u_ref[...] * bias_ref[...])


def outer_inv(pm, m, gate, u, bias_w, w):
    P, _, N1, C = pm.shape
    n_out = m.shape[0] // 2
    m_hi, m_lo = _split_hi_lo(m)
    bspec = pl.BlockSpec((2, n_out, w), lambda p, j: (p, 0, j))
    return pl.pallas_call(
        _outer_inv_kernel,
        grid=(P, C // w),
        in_specs=[
            pl.BlockSpec((1, 2, N1, w), lambda p, j: (p, 0, 0, j)),
            pl.BlockSpec(m.shape, lambda p, j: (0, 0)),
            pl.BlockSpec(m.shape, lambda p, j: (0, 0)),
            bspec, bspec,
            pl.BlockSpec((1, w), lambda p, j: (0, 0)),
        ],
        out_specs=bspec,
        out_shape=jax.ShapeDtypeStruct((2 * P, n_out, C), F32),
        compiler_params=_params(("parallel", "parallel")),
        name="hyena_dft_outer_inv",
    )(pm, m_hi, m_lo, gate, u, bias_w)


def _hyena_long_convs(v, x1, x2, k2, f_bias):
    B, L, D = v.shape
    N = 2 * L
    G = FFT_GROUP
    if N // G >= 16:
        N1, Nh = N // G, N // G // 2
        w = 8 * D
        fwd, fwd_real, inv = _outer_tables(N1, Nh)
        g_fwd, g_inv = _group_tables(N1)
        view = lambda t: t.reshape(B, Nh, G * D)
        z = v
        for n, gate in enumerate((x1, x2)):
            kf = group_spectrum(outer_fwd(k2[:, n].reshape(1, N1, G * D), fwd_real, 1, w).reshape(1, 2, N, D), g_fwd)
            a = outer_fwd(view(z), fwd, 2, w).reshape(B // 2, 2, N, D)
            pm = group_conv(a, kf, g_fwd, g_inv).reshape(B // 2, 2, N1, G * D)
            bias_w = jnp.tile(f_bias[n], w // D)[None]
            z = outer_inv(pm, inv, view(gate), view(z), bias_w, w).reshape(B, L, D)
        return z
    kk = jnp.arange(N, dtype=jnp.int32)
    c, s = _cos_sin(kk[:, None] * kk[None, :], N)
    g_real = jnp.concatenate([c, -s], 0)[None]
    ch, sh = c[:, :L], s[:, :L]
    g_fwd = jnp.concatenate([jnp.concatenate([ch, sh], 1), jnp.concatenate([-sh, ch], 1)], 0)[None]
    ci, si = ch.T, sh.T
    g_inv = (jnp.concatenate([jnp.concatenate([ci, -si], 1), jnp.concatenate([si, ci], 1)], 0) / N)[None]
    z = v
    for n, gate in enumerate((x1, x2)):
        kf = group_spectrum(k2[:, n].reshape(1, 1, N, D), g_real)
        y = group_conv(z.reshape(B // 2, 2, L, D), kf, g_fwd, g_inv).reshape(B, L, D)
        z = gate * (y + z * f_bias[n])
    return z


def _adaln(cond, w_mod, b_mod):
    m = jax.nn.silu(cond) @ w_mod + b_mod
    return jnp.split(m, 6, axis=-1)


def _hyena_filter_taps(L, f_w1, f_b1, f_w2, f_b2, f_w3, f_freq):
    t = jnp.linspace(0.0, 1.0, L, dtype=F32)[:, None]
    w = (2.0 * math.pi / L) * jnp.arange(L, dtype=F32)[:, None]
    f = jnp.linspace(1e-4, HY_BANDS - 1, HY_BANDS, dtype=F32)[None, :]
    z = jnp.concatenate([t, jnp.cos(f * w), -jnp.sin(f * w)], axis=-1)
    h = jnp.sin(f_freq[0] * (z @ f_w1 + f_b1))
    h = jnp.sin(f_freq[1] * (h @ f_w2 + f_b2))
    h = (h @ f_w3).reshape(L, HY_ORDER, 2, D_MODEL)
    deltas = jnp.abs(jnp.linspace(HY_MIN_DECAY, HY_MAX_DECAY, D_MODEL, dtype=F32))
    h = h * jnp.exp(-t * deltas)[:, None, None, :]
    fwd, bwd = h[:, :, 0], h[:, :, 1]
    k2 = jnp.concatenate([fwd, jnp.zeros((1, HY_ORDER, D_MODEL), F32), bwd[:0:-1]], axis=0)
    return k2 / (jnp.sum(jnp.abs(k2), axis=0, keepdims=True) + RMS_EPS)


def _hyena_core(p, conv_w, conv_b, f_w1, f_b1, f_w2, f_b2, f_w3, f_freq, f_bias):
    L = p.shape[1]
    vx = short_conv(p, conv_w, conv_b)
    k2 = _hyena_filter_taps(L, f_w1, f_b1, f_w2, f_b2, f_w3, f_freq)
    return _hyena_long_convs(vx[0], vx[1], vx[2], k2, f_bias)


def _axial_rope_tables(L):
    rows = L // GRID_W
    row = jnp.broadcast_to(jnp.arange(rows, dtype=F32)[:, None], (rows, GRID_W)).reshape(L)
    col = jnp.broadcast_to(jnp.arange(GRID_W, dtype=F32)[None, :], (rows, GRID_W)).reshape(L)
    n_freq = HEAD_DIM // 4
    inv_freq = ROPE_THETA ** (-jnp.arange(n_freq, dtype=F32) / n_freq)
    ang = jnp.concatenate([row[:, None] * inv_freq, col[:, None] * inv_freq], axis=-1)
    return jnp.cos(ang), jnp.sin(ang)


def _expert_choice_ffn(x, shift, scale, w_router, w_gate, w_up, w_down):
    B, n, D = x.shape
    cap = CAPACITY_FACTOR * n // N_EXPERTS
    h = x * (1.0 + scale) + shift
    aff = jax.nn.softmax(h @ w_router, axis=-1)
    gate, idx = lax.top_k(jnp.swapaxes(aff, 1, 2), cap)
    xg = gather_rows(h, idx).reshape(N_EXPERTS, B * cap, D).astype(BF16)
    gt = jnp.swapaxes(gate, 0, 1).reshape(N_EXPERTS, B * cap, 1)
    y = expert_ffn(xg, w_gate, w_up, w_down, gt, tf=512)
    return scatter_add_rows(y.reshape(N_EXPERTS, B, cap, D), idx, n)


def kernel(x, c, ctx, c_ctx, mod_w, mod_b, ln_g, ln_b, hy_w_in, hy_conv_w, hy_conv_b, hy_f_w1, hy_f_b1, hy_f_w2, hy_f_b2, hy_f_w3, hy_f_freq, hy_f_bias, hy_w_out, at_w_qkv, at_q_gain, at_k_gain, at_w_out, cm_w_in, cm_ln_g, cm_ln_b, cm_w_s, cm_b_s, cm_w_out, moe_router, moe_w_gate, moe_w_up, moe_w_down):
    B, L, D = x.shape
    Lc = ctx.shape[1]
    x_lat, x_ctx = x, ctx
    for i in range(DEPTH):
        kind, slot = LAYER_KINDS[i], KIND_SLOT[i]
        ctx_read = i <= LAST_CTX_LAYER
        ctx_update = i < LAST_CTX_LAYER
        sh1, sc1, g1, sh2, sc2, g2 = _adaln(c[:, None, :], mod_w[i], mod_b[i])
        if ctx_read:
            cmod = [jnp.broadcast_to(m[None, None, :], (B, 1, D)) for m in _adaln(c_ctx, mod_w[i], mod_b[i])]
        lg0, lb0, lg1, lb1 = ln_g[i, 0][None], ln_b[i, 0][None], ln_g[i, 1][None], ln_b[i, 1][None]

        if kind == 0:
            w_in = hy_w_in[slot].astype(BF16)
            w_out = hy_w_out[slot].astype(BF16)
            hy = (hy_conv_w[slot], hy_conv_b[slot], hy_f_w1[slot], hy_f_b1[slot], hy_f_w2[slot], hy_f_b2[slot],
                  hy_f_w3[slot], hy_f_freq[slot], hy_f_bias[slot])
            z = _hyena_core(mod_matmul(x_lat, sh1, sc1, w_in), *hy)
            x_lat = matmul_postnorm(z, w_out, x_lat, g1, lg0, lb0)
            if ctx_update:
                zc = _hyena_core(mod_matmul(x_ctx, cmod[0], cmod[1], w_in), *hy)
                x_ctx = matmul_postnorm(zc, w_out, x_ctx, cmod[2], lg0, lb0)
        elif kind == 1:
            w_qkv = at_w_qkv[slot].astype(BF16)
            w_out = at_w_out[slot].astype(BF16).reshape(N_Q_HEADS, HEAD_DIM, D)
            cos, sin = _axial_rope_tables(L)
            cos_t = jnp.tile(cos, (1, 128 // cos.shape[1]))
            sin_t = jnp.tile(jnp.concatenate([-sin, sin], axis=-1), (1, 128 // HEAD_DIM))
            qkv = mod_matmul(x_lat, sh1, sc1, w_qkv)
            q, k, v = qkv_prep(qkv, at_q_gain[slot], at_k_gain[slot], N_Q_HEADS, N_KV_HEADS, (cos_t, sin_t))
            kvc = mod_matmul(x_ctx, cmod[0], cmod[1], w_qkv[:, Q_COLS:])
            ck, cv = qkv_prep(kvc, None, at_k_gain[slot], 0, N_KV_HEADS, None)
            o = attention(q, ck, cv, k, v)
            x_lat = heads_matmul_postnorm(o, w_out, x_lat, g1, lg0, lb0)
        else:
            w_in = cm_w_in[slot].astype(BF16)
            w_out = cm_w_out[slot].astype(BF16)
            zz = jax.nn.gelu(mod_matmul(x_lat, sh1, sc1, w_in), approximate=False)
            u, v = jnp.split(zz, 2, axis=-1)
            mu = jnp.mean(v, axis=-1, keepdims=True)
            var = jnp.mean(jnp.square(v - mu), axis=-1, keepdims=True)
            v = (v - mu) * lax.rsqrt(var + LN_EPS) * cm_ln_g[slot] + cm_ln_b[slot]
            v = v.reshape(B, L // CHUNK, CHUNK, GMLP_GROUPS, GMLP_GROUP_W)
            v = jnp.einsum('gts,bnsgc->bntgc', cm_w_s[slot], v) + cm_b_s[slot].T[None, None, :, :, None]
            x_lat = matmul_postnorm(u * v.reshape(B, L, GMLP_HALF), w_out, x_lat, g1, lg0, lb0)

        moe = (moe_router[i], moe_w_gate[i], moe_w_up[i], moe_w_down[i])
        x_lat = postnorm(_expert_choice_ffn(x_lat, sh2, sc2, *moe), x_lat, g2, lg1, lb1)
        if ctx_update:
            x_ctx = postnorm(_expert_choice_ffn(x_ctx, cmod[3], cmod[4], *moe), x_ctx, cmod[5], lg1, lb1)
    return x_lat
```

```python
import math
from functools import partial

import jax
import jax.numpy as jnp
from jax import lax
from jax.experimental import pallas as pl
from jax.experimental.pallas import tpu as pltpu

F32 = jnp.float32
BF16 = jnp.bfloat16

D_MODEL = 1024
DEPTH = 4
GRID_W = 64
N_MIXERS = 3
LAYER_KINDS = tuple(i % N_MIXERS for i in range(DEPTH))
KIND_SLOT = tuple(LAYER_KINDS[:i].count(LAYER_KINDS[i]) for i in range(DEPTH))
LAST_CTX_LAYER = max([i for i, k in enumerate(LAYER_KINDS) if k == 1], default=-1)

HY_ORDER = 2
HY_BANDS = 16
HY_DECAY_TARGET = 1e-2
HY_FAST_DECAY_PCT = 0.3
HY_SLOW_DECAY_PCT = 1.5
HY_MIN_DECAY = math.log(HY_DECAY_TARGET) / HY_SLOW_DECAY_PCT
HY_MAX_DECAY = math.log(HY_DECAY_TARGET) / HY_FAST_DECAY_PCT

HEAD_DIM = 64
N_Q_HEADS = D_MODEL // HEAD_DIM
N_KV_HEADS = 4
GQA_GROUP = N_Q_HEADS // N_KV_HEADS
ROPE_THETA = 10000.0
ATTN_SCALE = HEAD_DIM ** -0.5
Q_COLS = N_Q_HEADS * HEAD_DIM

CHUNK = 128
GMLP_HALF = 2 * D_MODEL
GMLP_GROUPS = 16
GMLP_GROUP_W = GMLP_HALF // GMLP_GROUPS

N_EXPERTS = 16
CAPACITY_FACTOR = 2

RMS_EPS = 1e-6
LN_EPS = 1e-5
DEEPNORM_ALPHA = (2 * DEPTH) ** 0.25

VMEM_LIMIT_BYTES = 56 * 1024 * 1024
OUT_TILE_BYTES = 4 * 1024 * 1024


def _row_tile(rows, n_cols):
    tm = 512
    while tm > 8 and (tm * n_cols * 4 > OUT_TILE_BYTES or rows % tm):
        tm //= 2
    return tm


def _params(sem):
    return pltpu.CompilerParams(dimension_semantics=sem, vmem_limit_bytes=VMEM_LIMIT_BYTES)


def _gelu_erf(x):
    return 0.5 * x * (1.0 + lax.erf(x * (2.0 ** -0.5)))


def _mod_matmul_kernel(x_ref, sh_ref, sc_ref, w_ref, o_ref, *, gelu):
    h = x_ref[0] * (1.0 + sc_ref[0]) + sh_ref[0]
    y = jnp.dot(h.astype(BF16), w_ref[...], preferred_element_type=F32)
    o_ref[0] = _gelu_erf(y) if gelu else y


def mod_matmul(x, shift, scale, w, gelu=False):
    B, L, K = x.shape
    N = w.shape[1]
    tm = _row_tile(L, N)
    return pl.pallas_call(
        partial(_mod_matmul_kernel, gelu=gelu),
        grid=(B, L // tm),
        in_specs=[
            pl.BlockSpec((1, tm, K), lambda b, i: (b, i, 0)),
            pl.BlockSpec((1, 1, K), lambda b, i: (b, 0, 0)),
            pl.BlockSpec((1, 1, K), lambda b, i: (b, 0, 0)),
            pl.BlockSpec((K, N), lambda b, i: (0, 0)),
        ],
        out_specs=pl.BlockSpec((1, tm, N), lambda b, i: (b, i, 0)),
        out_shape=jax.ShapeDtypeStruct((B, L, N), F32),
        compiler_params=_params(("parallel", "parallel")),
        name="mod_matmul",
    )(x, shift, scale, w)


def _layer_norm_rows(t, g, b):
    mu = jnp.mean(t, axis=-1, keepdims=True)
    d = t - mu
    var = jnp.mean(d * d, axis=-1, keepdims=True)
    return d * lax.rsqrt(var + LN_EPS) * g + b


def _matmul_postnorm_kernel(z_ref, w_ref, x_ref, gate_ref, g_ref, b_ref, o_ref):
    y = jnp.dot(z_ref[0].astype(BF16), w_ref[...], preferred_element_type=F32)
    t = DEEPNORM_ALPHA * x_ref[0] + gate_ref[0] * y
    o_ref[0] = _layer_norm_rows(t, g_ref[...], b_ref[...])


def matmul_postnorm(z, w, x, gate, g, b):
    B, L, K = z.shape
    D = w.shape[1]
    tm = _row_tile(L, max(K, D))
    return pl.pallas_call(
        _matmul_postnorm_kernel,
        grid=(B, L // tm),
        in_specs=[
            pl.BlockSpec((1, tm, K), lambda bi, i: (bi, i, 0)),
            pl.BlockSpec((K, D), lambda bi, i: (0, 0)),
            pl.BlockSpec((1, tm, D), lambda bi, i: (bi, i, 0)),
            pl.BlockSpec((1, 1, D), lambda bi, i: (bi, 0, 0)),
            pl.BlockSpec((1, D), lambda bi, i: (0, 0)),
            pl.BlockSpec((1, D), lambda bi, i: (0, 0)),
        ],
        out_specs=pl.BlockSpec((1, tm, D), lambda bi, i: (bi, i, 0)),
        out_shape=jax.ShapeDtypeStruct((B, L, D), F32),
        compiler_params=_params(("parallel", "parallel")),
        name="matmul_postnorm",
    )(z, w, x, gate, g, b)


def _postnorm_kernel(y_ref, x_ref, gate_ref, g_ref, b_ref, o_ref):
    t = DEEPNORM_ALPHA * x_ref[0] + gate_ref[0] * y_ref[0]
    o_ref[0] = _layer_norm_rows(t, g_ref[...], b_ref[...])


def postnorm(y, x, gate, g, b):
    B, L, D = x.shape
    tm = _row_tile(L, D)
    return pl.pallas_call(
        _postnorm_kernel,
        grid=(B, L // tm),
        in_specs=[
            pl.BlockSpec((1, tm, D), lambda bi, i: (bi, i, 0)),
            pl.BlockSpec((1, tm, D), lambda bi, i: (bi, i, 0)),
            pl.BlockSpec((1, 1, D), lambda bi, i: (bi, 0, 0)),
            pl.BlockSpec((1, D), lambda bi, i: (0, 0)),
            pl.BlockSpec((1, D), lambda bi, i: (0, 0)),
        ],
        out_specs=pl.BlockSpec((1, tm, D), lambda bi, i: (bi, i, 0)),
        out_shape=jax.ShapeDtypeStruct((B, L, D), F32),
        compiler_params=_params(("parallel", "parallel")),
        name="postnorm",
    )(y, x, gate, g, b)


def _gmlp_tail_kernel(z_ref, lng_ref, lnb_ref, ws_ref, bs_ref, w_ref, x_ref, gate_ref, g_ref, b_ref, o_ref):
    z = z_ref[0]
    tm, half = z.shape[0], z.shape[1] // 2
    u, v = z[:, :half], z[:, half:]
    vn = _layer_norm_rows(v, lng_ref[...], lnb_ref[...]).astype(BF16)
    n_grp = ws_ref.shape[0]
    gw = half // n_grp
    chunks = []
    for c in range(tm // CHUNK):
        cols = []
        for g in range(n_grp):
            blk = vn[c * CHUNK:(c + 1) * CHUNK, g * gw:(g + 1) * gw]
            cols.append(jnp.dot(ws_ref[g], blk, preferred_element_type=F32) + bs_ref[:, g:g + 1])
        chunks.append(jnp.concatenate(cols, axis=1))
    sv = jnp.concatenate(chunks, axis=0)
    y = jnp.dot((u * sv).astype(BF16), w_ref[...], preferred_element_type=F32)
    t = DEEPNORM_ALPHA * x_ref[0] + gate_ref[0] * y
    o_ref[0] = _layer_norm_rows(t, g_ref[...], b_ref[...])


def gmlp_tail(z, ln_g, ln_b, w_s, b_s_t, w_out, x, gate, g, b, tm=256):
    B, L, H2 = z.shape
    H, D = w_out.shape
    const = lambda shape: pl.BlockSpec(shape, lambda bi, i: (0,) * len(shape))
    return pl.pallas_call(
        _gmlp_tail_kernel,
        grid=(B, L // tm),
        in_specs=[
            pl.BlockSpec((1, tm, H2), lambda bi, i: (bi, i, 0)),
            const((1, H)), const((1, H)), const(w_s.shape), const(b_s_t.shape), const((H, D)),
            pl.BlockSpec((1, tm, D), lambda bi, i: (bi, i, 0)),
            pl.BlockSpec((1, 1, D), lambda bi, i: (bi, 0, 0)),
            const((1, D)), const((1, D)),
        ],
        out_specs=pl.BlockSpec((1, tm, D), lambda bi, i: (bi, i, 0)),
        out_shape=jax.ShapeDtypeStruct((B, L, D), F32),
        compiler_params=_params(("parallel", "parallel")),
        name="gmlp_tail",
    )(z, ln_g, ln_b, w_s, b_s_t, w_out, x, gate, g, b)


def _head_rms_norm(x, ones_bd, gain):
    sq = x * x
    sq_hi = sq.astype(BF16)
    sq_lo = (sq - sq_hi.astype(F32)).astype(BF16)
    ss = jnp.dot(sq_hi, ones_bd, preferred_element_type=F32) + jnp.dot(sq_lo, ones_bd, preferred_element_type=F32)
    return x * lax.rsqrt(ss * (1.0 / HEAD_DIM) + RMS_EPS) * gain


def _rope(x, cos_t, sin_t):
    w = x.shape[1]
    lane = lax.broadcasted_iota(jnp.int32, x.shape, 1)
    half = HEAD_DIM // 2
    partner = jnp.where((lane & (HEAD_DIM - 1)) < half, pltpu.roll(x, w - half, 1), pltpu.roll(x, half, 1))
    reps = w // cos_t.shape[1]
    return x * jnp.tile(cos_t, (1, reps)) + partner * jnp.tile(sin_t, (1, reps))


def _qkv_prep_kernel(*refs, n_q, rope):
    refs = list(refs)
    x_ref = refs.pop(0)
    cos_ref, sin_ref = (refs.pop(0), refs.pop(0)) if rope else (None, None)
    if n_q:
        qg_ref, qbd_ref = refs.pop(0), refs.pop(0)
    kg_ref, kbd_ref = refs.pop(0), refs.pop(0)
    if n_q:
        q_out = refs.pop(0)
    k_out, v_out = refs
    x = x_ref[0]
    n_kv = k_out.shape[1]
    qc, kc = n_q * HEAD_DIM, n_kv * HEAD_DIM
    if n_q:
        q = _head_rms_norm(x[:, :qc], qbd_ref[...], qg_ref[...])
        if rope:
            q = _rope(q, cos_ref[...], sin_ref[...])
        q = q * ATTN_SCALE
        for h in range(n_q):
            q_out[0, h] = q[:, h * HEAD_DIM:(h + 1) * HEAD_DIM].astype(BF16)
    k = _head_rms_norm(x[:, qc:qc + kc], kbd_ref[...], kg_ref[...])
    if rope:
        k = _rope(k, cos_ref[...], sin_ref[...])
    v = x[:, qc + kc:qc + 2 * kc]
    for h in range(n_kv):
        k_out[0, h] = k[:, h * HEAD_DIM:(h + 1) * HEAD_DIM].astype(BF16)
        v_out[0, h] = v[:, h * HEAD_DIM:(h + 1) * HEAD_DIM].astype(BF16)


def qkv_prep(x, q_gain, k_gain, n_q, n_kv, rope_tables, tm=256):
    B, L, C = x.shape
    hd = HEAD_DIM
    tm = min(tm, L)

    def ones_bd(n):
        r = jnp.arange(n * hd) // hd
        return (r[:, None] == r[None, :]).astype(BF16)

    const = lambda shape: pl.BlockSpec(shape, lambda b, i: (0,) * len(shape))
    args, specs = [x], [pl.BlockSpec((1, tm, C), lambda b, i: (b, i, 0))]
    if rope_tables is not None:
        args += list(rope_tables)
        specs += [pl.BlockSpec((tm, 128), lambda b, i: (i, 0))] * 2
    if n_q:
        args += [jnp.tile(q_gain, n_q)[None], ones_bd(n_q)]
        specs += [const((1, n_q * hd)), const((n_q * hd, n_q * hd))]
    args += [jnp.tile(k_gain, n_kv)[None], ones_bd(n_kv)]
    specs += [const((1, n_kv * hd)), const((n_kv * hd, n_kv * hd))]
    head_spec = lambda n: pl.BlockSpec((1, n, tm, hd), lambda b, i: (b, 0, i, 0))
    head_shape = lambda n: jax.ShapeDtypeStruct((B, n, L, hd), BF16)
    out_specs = ([head_spec(n_q)] if n_q else []) + [head_spec(n_kv), head_spec(n_kv)]
    out_shape = ([head_shape(n_q)] if n_q else []) + [head_shape(n_kv), head_shape(n_kv)]
    return pl.pallas_call(
        partial(_qkv_prep_kernel, n_q=n_q, rope=rope_tables is not None),
        grid=(B, L // tm),
        in_specs=specs,
        out_specs=out_specs,
        out_shape=out_shape,
        compiler_params=_params(("parallel", "parallel")),
        name="qkv_prep",
    )(*args)


def _nt_dot(a, b):
    return lax.dot_general(a, b, (((1,), (1,)), ((), ())), preferred_element_type=F32)


def _attention_kernel(q_ref, ck_ref, cv_ref, k_ref, v_ref, o_ref):
    g, tq, hd = q_ref.shape[1], q_ref.shape[2], q_ref.shape[3]
    q = q_ref[0].reshape(g * tq, hd)
    s_c = _nt_dot(q, ck_ref[0, 0])
    s_l = _nt_dot(q, k_ref[0, 0])
    m = jnp.maximum(jnp.max(s_c, axis=-1, keepdims=True), jnp.max(s_l, axis=-1, keepdims=True))
    p_c = jnp.exp(s_c - m)
    p_l = jnp.exp(s_l - m)
    l = jnp.sum(p_c, axis=-1, keepdims=True) + jnp.sum(p_l, axis=-1, keepdims=True)
    o = (jnp.dot(p_c.astype(BF16), cv_ref[0, 0], preferred_element_type=F32)
         + jnp.dot(p_l.astype(BF16), v_ref[0, 0], preferred_element_type=F32)) / l
    o_ref[0] = o.astype(BF16).reshape(g, tq, hd)


def attention(q, ck, cv, k, v, tq=128):
    B, H, L, hd = q.shape
    KV, Lc = ck.shape[1], ck.shape[2]
    G = H // KV
    kv_spec = lambda n: pl.BlockSpec((1, 1, n, hd), lambda b, kv, i: (b, kv, 0, 0))
    q_spec = pl.BlockSpec((1, G, tq, hd), lambda b, kv, i: (b, kv, i, 0))
    return pl.pallas_call(
        _attention_kernel,
        grid=(B, KV, L // tq),
        in_specs=[q_spec, kv_spec(Lc), kv_spec(Lc), kv_spec(L), kv_spec(L)],
        out_specs=q_spec,
        out_shape=jax.ShapeDtypeStruct((B, H, L, hd), BF16),
        compiler_params=_params(("parallel", "parallel", "parallel")),
        name="gqa_attention",
    )(q, ck, cv, k, v)


def _heads_matmul_postnorm_kernel(o_ref, w_ref, x_ref, gate_ref, g_ref, b_ref, out_ref):
    y = jnp.dot(o_ref[0, 0], w_ref[0], preferred_element_type=F32)
    for h in range(1, o_ref.shape[1]):
        y += jnp.dot(o_ref[0, h], w_ref[h], preferred_element_type=F32)
    t = DEEPNORM_ALPHA * x_ref[0] + gate_ref[0] * y
    out_ref[0] = _layer_norm_rows(t, g_ref[...], b_ref[...])


def heads_matmul_postnorm(o, w, x, gate, g, b, tm=512):
    B, H, L, hd = o.shape
    D = w.shape[2]
    return pl.pallas_call(
        _heads_matmul_postnorm_kernel,
        grid=(B, L // tm),
        in_specs=[
            pl.BlockSpec((1, H, tm, hd), lambda bi, i: (bi, 0, i, 0)),
            pl.BlockSpec((H, hd, D), lambda bi, i: (0, 0, 0)),
            pl.BlockSpec((1, tm, D), lambda bi, i: (bi, i, 0)),
            pl.BlockSpec((1, 1, D), lambda bi, i: (bi, 0, 0)),
            pl.BlockSpec((1, D), lambda bi, i: (0, 0)),
            pl.BlockSpec((1, D), lambda bi, i: (0, 0)),
        ],
        out_specs=pl.BlockSpec((1, tm, D), lambda bi, i: (bi, i, 0)),
        out_shape=jax.ShapeDtypeStruct((B, L, D), F32),
        compiler_params=_params(("parallel", "parallel")),
        name="heads_matmul_postnorm",
    )(o, w, x, gate, g, b)


def _router_kernel(x_ref, sh_ref, sc_ref, w_ref, o_ref):
    h = (x_ref[0] * (1.0 + sc_ref[0]) + sh_ref[0]).astype(BF16)
    logits = _nt_dot(w_ref[...], h)
    e = jnp.exp(logits - jnp.max(logits, axis=0, keepdims=True))
    o_ref[0] = e / jnp.sum(e, axis=0, keepdims=True)


def router(x, shift, scale, w_t):
    B, n, D = x.shape
    E = w_t.shape[0]
    tm = min(n, 512)
    vec = pl.BlockSpec((1, 1, D), lambda b, i: (b, 0, 0))
    return pl.pallas_call(
        _router_kernel,
        grid=(B, n // tm),
        in_specs=[pl.BlockSpec((1, tm, D), lambda b, i: (b, i, 0)), vec, vec, pl.BlockSpec((E, D), lambda b, i: (0, 0))],
        out_specs=pl.BlockSpec((1, E, tm), lambda b, i: (b, 0, i)),
        out_shape=jax.ShapeDtypeStruct((B, E, n), F32),
        compiler_params=_params(("parallel", "parallel")),
        name="moe_router",
    )(x, shift, scale, w_t)


ROUTE_UNROLL = 8


def _gather_rows_kernel(idx_ref, x_ref, sh_ref, sc_ref, o_ref, buf):
    cap = buf.shape[0]
    base = (pl.program_id(0) * pl.num_programs(1) + pl.program_id(1)) * cap

    def body(s, carry):
        for u in range(ROUTE_UNROLL):
            r = s * ROUTE_UNROLL + u
            buf[pl.ds(r, 1), :] = x_ref[pl.ds(idx_ref[base + r], 1), :]
        return carry

    lax.fori_loop(0, cap // ROUTE_UNROLL, body, 0)
    o_ref[...] = (buf[...] * (1.0 + sc_ref[...]) + sh_ref[...]).astype(BF16)


def gather_rows(x, shift, scale, idx):
    B, n, D = x.shape
    _, E, cap = idx.shape
    vec = pl.BlockSpec((None, 1, D), lambda b, e, idx: (b, 0, 0))
    return pl.pallas_call(
        _gather_rows_kernel,
        grid_spec=pltpu.PrefetchScalarGridSpec(
            num_scalar_prefetch=1,
            grid=(B, E),
            in_specs=[pl.BlockSpec((None, n, D), lambda b, e, idx: (b, 0, 0)), vec, vec],
            out_specs=pl.BlockSpec((None, None, cap, D), lambda b, e, idx: (e, b, 0, 0)),
            scratch_shapes=[pltpu.VMEM((cap, D), F32)],
        ),
        out_shape=jax.ShapeDtypeStruct((E, B, cap, D), BF16),
        compiler_params=_params(("parallel", "arbitrary")),
        name="moe_gather_rows",
    )(idx.reshape(-1), x, shift, scale)


def _scatter_add_kernel(idx_ref, y_ref, o_ref):
    cap = y_ref.shape[0]
    base = (pl.program_id(0) * pl.num_programs(1) + pl.program_id(1)) * cap

    @pl.when(pl.program_id(1) == 0)
    def _():
        o_ref[...] = jnp.zeros(o_ref.shape, o_ref.dtype)

    def body(s, carry):
        rows = [idx_ref[base + s * ROUTE_UNROLL + u] for u in range(ROUTE_UNROLL)]
        sums = [o_ref[pl.ds(rows[u], 1), :] + y_ref[pl.ds(s * ROUTE_UNROLL + u, 1), :] for u in range(ROUTE_UNROLL)]
        for u in range(ROUTE_UNROLL):
            o_ref[pl.ds(rows[u], 1), :] = sums[u]
        return carry

    lax.fori_loop(0, cap // ROUTE_UNROLL, body, 0)


def scatter_add_rows(y, idx, n):
    E, R, D = y.shape
    B, _, cap = idx.shape
    return pl.pallas_call(
        _scatter_add_kernel,
        grid_spec=pltpu.PrefetchScalarGridSpec(
            num_scalar_prefetch=1,
            grid=(B, E),
            in_specs=[pl.BlockSpec((None, cap, D), lambda b, e, idx: (e, b, 0))],
            out_specs=pl.BlockSpec((None, n, D), lambda b, e, idx: (b, 0, 0)),
        ),
        out_shape=jax.ShapeDtypeStruct((B, n, D), F32),
        compiler_params=_params(("parallel", "arbitrary")),
        name="moe_scatter_add_rows",
    )(idx.reshape(-1), y)


def _expert_ffn_kernel(x_ref, wg_ref, wu_ref, wd_ref, gate_ref, o_ref):
    f = pl.program_id(1)
    x = x_ref[0]
    a = jnp.dot(x, wg_ref[0].astype(BF16), preferred_element_type=F32)
    u = jnp.dot(x, wu_ref[0].astype(BF16), preferred_element_type=F32)
    h = (a * jax.nn.sigmoid(a) * u).astype(BF16)
    y = jnp.dot(h, wd_ref[0].astype(BF16), preferred_element_type=F32)

    @pl.when(f == 0)
    def _():
        o_ref[0] = y

    @pl.when(f > 0)
    def _():
        o_ref[0] += y

    @pl.when(f == pl.num_programs(1) - 1)
    def _():
        o_ref[0] *= gate_ref[0]


def expert_ffn(xg, w_gate, w_up, w_down, layer, gate, tf=512):
    E, R, D = xg.shape
    F = w_gate.shape[3]
    return pl.pallas_call(
        _expert_ffn_kernel,
        grid=(E, F // tf),
        in_specs=[
            pl.BlockSpec((1, R, D), lambda e, f: (e, 0, 0)),
            pl.BlockSpec((None, 1, D, tf), lambda e, f: (layer, e, 0, f)),
            pl.BlockSpec((None, 1, D, tf), lambda e, f: (layer, e, 0, f)),
            pl.BlockSpec((None, 1, tf, D), lambda e, f: (layer, e, f, 0)),
            pl.BlockSpec((1, R, 1), lambda e, f: (e, 0, 0)),
        ],
        out_specs=pl.BlockSpec((1, R, D), lambda e, f: (e, 0, 0)),
        out_shape=jax.ShapeDtypeStruct((E, R, D), F32),
        compiler_params=_params(("parallel", "arbitrary")),
        name="expert_ffn",
    )(xg, w_gate, w_up, w_down, gate)


def _short_conv_kernel(p_ref, w_ref, b_ref, o_ref):
    q = p_ref[0]
    L = q.shape[0]
    row = lax.broadcasted_iota(jnp.int32, q.shape, 0)
    prev = jnp.where(row == 0, 0.0, pltpu.roll(q, 1, 0))
    nxt = jnp.where(row == L - 1, 0.0, pltpu.roll(q, L - 1, 0))
    w = w_ref[...]
    o_ref[0, 0] = prev * w[0:1] + q * w[1:2] + nxt * w[2:3] + b_ref[...]


def short_conv(p, conv_w, conv_b, tc=256):
    B, L, D3 = p.shape
    D = D3 // 3
    nj = D // tc
    return pl.pallas_call(
        _short_conv_kernel,
        grid=(3, B, nj),
        in_specs=[
            pl.BlockSpec((1, L, tc), lambda s, b, j: (b, 0, s * nj + j)),
            pl.BlockSpec((3, tc), lambda s, b, j: (0, s * nj + j)),
            pl.BlockSpec((1, tc), lambda s, b, j: (0, s * nj + j)),
        ],
        out_specs=pl.BlockSpec((1, 1, L, tc), lambda s, b, j: (s, b, 0, j)),
        out_shape=jax.ShapeDtypeStruct((3, B, L, D), F32),
        compiler_params=_params(("parallel", "parallel", "parallel")),
        name="hyena_short_conv",
    )(p, conv_w, conv_b[None])


FFT_GROUP = 128
SUBLANES = 8
FFT_DATA_PASSES = 1
FFT_FILTER_PASSES = 3


def _split_hi_lo(m):
    hi = m.astype(BF16)
    return hi, (m - hi.astype(F32)).astype(BF16)


def _dot_split(m_hi, m_lo, x, passes):
    x_hi = x.astype(BF16)
    acc = jnp.dot(m_hi, x_hi, preferred_element_type=F32)
    if passes >= 2:
        x_lo = (x - x_hi.astype(F32)).astype(BF16)
        acc += jnp.dot(m_hi, x_lo, preferred_element_type=F32)
    if passes >= 3:
        acc += jnp.dot(m_lo, x_hi, preferred_element_type=F32)
    return acc


def _cos_sin(num, den):
    ang = (2.0 * math.pi / den) * (num % den).astype(F32)
    return jnp.cos(ang), jnp.sin(ang)


def _outer_tables(N1, n_in):
    k1 = jnp.arange(N1, dtype=jnp.int32)
    c, s = _cos_sin(k1[:, None] * jnp.arange(n_in, dtype=jnp.int32)[None, :], N1)
    fwd = jnp.concatenate([jnp.concatenate([c, s], 1), jnp.concatenate([-s, c], 1)], 0)
    cf, sf = _cos_sin(k1[:, None] * k1[None, :], N1)
    fwd_real = jnp.concatenate([cf, -sf], 0)
    ci, si = c.T, s.T
    inv = jnp.concatenate([jnp.concatenate([ci, -si], 1), jnp.concatenate([si, ci], 1)], 0) / (N1 * FFT_GROUP)
    return fwd, fwd_real, inv


def _group_tables(N1):
    G = FFT_GROUP
    k = jnp.arange(N1, dtype=jnp.int32)[:, None, None] + N1 * jnp.arange(G, dtype=jnp.int32)[None, :, None]
    gr, gs = _cos_sin(k * jnp.arange(G, dtype=jnp.int32)[None, None, :], N1 * G)
    gi = -gs
    fwd = jnp.concatenate([jnp.concatenate([gr, -gi], 2), jnp.concatenate([gi, gr], 2)], 1)
    return fwd, jnp.swapaxes(fwd, 1, 2)


def _tile_expand(m):
    r, c = m.shape
    eye = jnp.eye(SUBLANES, dtype=m.dtype)
    return (m[:, None, :, None] * eye[None, :, None, :]).reshape(r * SUBLANES, c * SUBLANES)


def _outer_fwd_kernel(z_ref, m_hi, m_lo, o_ref, *, passes):
    parts, n_in, sub, d = z_ref.shape
    x = z_ref[...].reshape(parts * n_in * sub, d)
    r = _dot_split(m_hi[...], m_lo[...], x, passes)
    o_ref[0] = r.reshape(2, r.shape[0] // (2 * sub), sub, d)


def outer_fwd(z, m, parts, first, n_seq):
    _, n_in, G, D = z.shape
    N1 = m.shape[0] // 2
    m_hi, m_lo = _split_hi_lo(_tile_expand(m))
    first_blk = first // parts
    return pl.pallas_call(
        partial(_outer_fwd_kernel, passes=FFT_DATA_PASSES if parts == 2 else FFT_FILTER_PASSES),
        grid=(n_seq, G // SUBLANES),
        in_specs=[
            pl.BlockSpec((parts, n_in, SUBLANES, D), lambda p, j: (first_blk + p, 0, j, 0)),
            pl.BlockSpec(m_hi.shape, lambda p, j: (0, 0)),
            pl.BlockSpec(m_hi.shape, lambda p, j: (0, 0)),
        ],
        out_specs=pl.BlockSpec((1, 2, N1, SUBLANES, D), lambda p, j: (p, 0, 0, j, 0)),
        out_shape=jax.ShapeDtypeStruct((n_seq, 2, N1, G, D), F32),
        compiler_params=_params(("parallel", "parallel")),
        name="hyena_dft_outer_fwd",
    )(z, m_hi, m_lo)


def _group_spectrum_kernel(a_ref, g_hi, g_lo, o_ref):
    a = a_ref[0]
    x = _dot_split(g_hi[0], g_lo[0], a.reshape(a.shape[0] * a.shape[1], a.shape[2]), FFT_FILTER_PASSES)
    o_ref[...] = x.reshape(2, x.shape[0] // 2, x.shape[1])


def group_spectrum(a, g):
    _, parts, rows, D = a.shape
    n_grp, Rm2 = g.shape[0], g.shape[1]
    Rin = rows // n_grp
    g_hi, g_lo = _split_hi_lo(g)
    return pl.pallas_call(
        _group_spectrum_kernel,
        grid=(n_grp,),
        in_specs=[
            pl.BlockSpec((1, parts, Rin, D), lambda k: (0, 0, k, 0)),
            pl.BlockSpec((1,) + g.shape[1:], lambda k: (k, 0, 0)),
            pl.BlockSpec((1,) + g.shape[1:], lambda k: (k, 0, 0)),
        ],
        out_specs=pl.BlockSpec((2, Rm2 // 2, D), lambda k: (0, k, 0)),
        out_shape=jax.ShapeDtypeStruct((2, n_grp * Rm2 // 2, D), F32),
        compiler_params=_params(("parallel",)),
        name="hyena_filter_spectrum",
    )(a, g_hi, g_lo)


def _group_conv_kernel(a_ref, kf_ref, g_hi, g_lo, h_hi, h_lo, o_ref):
    a = a_ref[0]
    x = _dot_split(g_hi[0], g_lo[0], a.reshape(a.shape[0] * a.shape[1], a.shape[2]), FFT_DATA_PASSES)
    r = x.shape[0] // 2
    xr, xi = x[:r], x[r:]
    kr, ki = kf_ref[0], kf_ref[1]
    y = jnp.concatenate([xr * kr - xi * ki, xr * ki + xi * kr], axis=0)
    p = _dot_split(h_hi[0], h_lo[0], y, FFT_DATA_PASSES)
    o_ref[0] = p.reshape(2, p.shape[0] // 2, p.shape[1])


def group_conv(a, kf, g, h):
    P, _, rows, D = a.shape
    n_grp, Rm2, Rout2 = g.shape[0], g.shape[1], h.shape[1]
    Rin = rows // n_grp
    g_hi, g_lo = _split_hi_lo(g)
    h_hi, h_lo = _split_hi_lo(h)
    gspec = pl.BlockSpec((1,) + g.shape[1:], lambda k, p: (k, 0, 0))
    hspec = pl.BlockSpec((1,) + h.shape[1:], lambda k, p: (k, 0, 0))
    return pl.pallas_call(
        _group_conv_kernel,
        grid=(n_grp, P),
        in_specs=[
            pl.BlockSpec((1, 2, Rin, D), lambda k, p: (p, 0, k, 0)),
            pl.BlockSpec((2, Rm2 // 2, D), lambda k, p: (0, k, 0)),
            gspec, gspec, hspec, hspec,
        ],
        out_specs=pl.BlockSpec((1, 2, Rout2 // 2, D), lambda k, p: (p, 0, k, 0)),
        out_shape=jax.ShapeDtypeStruct((P, 2, n_grp * Rout2 // 2, D), F32),
        compiler_params=_params(("parallel", "parallel")),
        name="hyena_group_conv",
    )(a, kf, g_hi, g_lo, h_hi, h_lo)


def _outer_inv_kernel(p_ref, m_hi, m_lo, gate_ref, u_ref, bias_ref, o_ref):
    _, _, n1, sub, d = p_ref.shape
    y = _dot_split(m_hi[...], m_lo[...], p_ref[0].reshape(2 * n1 * sub, d), FFT_DATA_PASSES)
    y = y.reshape(2, y.shape[0] // (2 * sub), sub, d)
    o_ref[...] = gate_ref[...] * (y + u_ref[...] * bias_ref[...])


def outer_inv(pm, m, gate, gate_first, u, u_first, bias):
    P, _, N1, G, D = pm.shape
    n_out = m.shape[0] // 2
    m_hi, m_lo = _split_hi_lo(_tile_expand(m))
    seq = lambda first: pl.BlockSpec((2, n_out, SUBLANES, D), lambda p, j: (first // 2 + p, 0, j, 0))
    return pl.pallas_call(
        _outer_inv_kernel,
        grid=(P, G // SUBLANES),
        in_specs=[
            pl.BlockSpec((1, 2, N1, SUBLANES, D), lambda p, j: (p, 0, 0, j, 0)),
            pl.BlockSpec(m_hi.shape, lambda p, j: (0, 0)),
            pl.BlockSpec(m_hi.shape, lambda p, j: (0, 0)),
            seq(gate_first), seq(u_first),
            pl.BlockSpec((1, D), lambda p, j: (0, 0)),
        ],
        out_specs=seq(0),
        out_shape=jax.ShapeDtypeStruct((2 * P, n_out, G, D), F32),
        compiler_params=_params(("parallel", "parallel")),
        name="hyena_dft_outer_inv",
    )(pm, m_hi, m_lo, gate, u, bias)


def _hyena_long_convs(vx, k2, f_bias):
    _, B, L, D = vx.shape
    N = 2 * L
    G = FFT_GROUP
    if N // G >= 16:
        N1, Nh = N // G, N // G // 2
        fwd, fwd_real, inv = _outer_tables(N1, Nh)
        g_fwd, g_inv = _group_tables(N1)
        vx4 = vx.reshape(3 * B, Nh, G, D)
        k4 = k2.reshape(HY_ORDER, N1, G, D)
        z, z_first = vx4, 0
        for n in range(HY_ORDER):
            kf = group_spectrum(outer_fwd(k4, fwd_real, 1, n, 1).reshape(1, 2, N, D), g_fwd)
            a = outer_fwd(z, fwd, 2, z_first, B // 2).reshape(B // 2, 2, N, D)
            pm = group_conv(a, kf, g_fwd, g_inv).reshape(B // 2, 2, N1, G, D)
            z = outer_inv(pm, inv, vx4, (n + 1) * B, z, z_first, f_bias[n][None])
            z_first = 0
        return z.reshape(B, L, D)
    kk = jnp.arange(N, dtype=jnp.int32)
    c, s = _cos_sin(kk[:, None] * kk[None, :], N)
    g_real = jnp.concatenate([c, -s], 0)[None]
    ch, sh = c[:, :L], s[:, :L]
    g_fwd = jnp.concatenate([jnp.concatenate([ch, sh], 1), jnp.concatenate([-sh, ch], 1)], 0)[None]
    ci, si = ch.T, sh.T
    g_inv = (jnp.concatenate([jnp.concatenate([ci, -si], 1), jnp.concatenate([si, ci], 1)], 0) / N)[None]
    z = vx[0]
    for n in range(HY_ORDER):
        kf = group_spectrum(k2[n].reshape(1, 1, N, D), g_real)
        y = group_conv(z.reshape(B // 2, 2, L, D), kf, g_fwd, g_inv).reshape(B, L, D)
        z = vx[n + 1] * (y + z * f_bias[n])
    return z


def _adaln(cond, w_mod, b_mod):
    m = jax.nn.silu(cond) @ w_mod + b_mod
    return jnp.split(m, 6, axis=-1)


def _hyena_filter_taps(L, f_w1, f_b1, f_w2, f_b2, f_w3, f_freq):
    row = jnp.arange(2 * L, dtype=jnp.int32)
    pos = jnp.where(row < L, row, 2 * L - row)
    t = jnp.linspace(0.0, 1.0, L, dtype=F32)[jnp.minimum(pos, L - 1)][:, None]
    w = (2.0 * math.pi / L) * pos.astype(F32)[:, None]
    f = jnp.linspace(1e-4, HY_BANDS - 1, HY_BANDS, dtype=F32)[None, :]
    z = jnp.concatenate([t, jnp.cos(f * w), -jnp.sin(f * w)], axis=-1)
    h = jnp.sin(f_freq[0] * (z @ f_w1 + f_b1))
    h = jnp.sin(f_freq[1] * (h @ f_w2 + f_b2))
    h = (h @ f_w3).reshape(2 * L, HY_ORDER, 2, D_MODEL)
    deltas = jnp.abs(jnp.linspace(HY_MIN_DECAY, HY_MAX_DECAY, D_MODEL, dtype=F32))
    window = jnp.where(row[:, None] == L, 0.0, jnp.exp(-t * deltas))
    taps = []
    for n in range(HY_ORDER):
        k = jnp.where(row[:, None] < L, h[:, n, 0], h[:, n, 1]) * window
        taps.append(k / (jnp.sum(jnp.abs(k), axis=0, keepdims=True) + RMS_EPS))
    return jnp.stack(taps)


def _hyena_core(p, conv_w, conv_b, f_w1, f_b1, f_w2, f_b2, f_w3, f_freq, f_bias):
    L = p.shape[1]
    vx = short_conv(p, conv_w, conv_b)
    k2 = _hyena_filter_taps(L, f_w1, f_b1, f_w2, f_b2, f_w3, f_freq)
    return _hyena_long_convs(vx, k2, f_bias)


def _axial_rope_tables(L):
    rows = L // GRID_W
    row = jnp.broadcast_to(jnp.arange(rows, dtype=F32)[:, None], (rows, GRID_W)).reshape(L)
    col = jnp.broadcast_to(jnp.arange(GRID_W, dtype=F32)[None, :], (rows, GRID_W)).reshape(L)
    n_freq = HEAD_DIM // 4
    inv_freq = ROPE_THETA ** (-jnp.arange(n_freq, dtype=F32) / n_freq)
    ang = jnp.concatenate([row[:, None] * inv_freq, col[:, None] * inv_freq], axis=-1)
    return jnp.cos(ang), jnp.sin(ang)


def _expert_choice_ffn(x, shift, scale, w_router_t, w_gate, w_up, w_down, layer):
    B, n, D = x.shape
    cap = CAPACITY_FACTOR * n // N_EXPERTS
    aff = router(x, shift, scale, w_router_t)
    gate, idx = lax.top_k(aff, cap)
    xg = gather_rows(x, shift, scale, idx).reshape(N_EXPERTS, B * cap, D)
    gt = jnp.swapaxes(gate, 0, 1).reshape(N_EXPERTS, B * cap, 1)
    y = expert_ffn(xg, w_gate, w_up, w_down, layer, gt)
    return scatter_add_rows(y, idx, n)


def kernel(x, c, ctx, c_ctx, mod_w, mod_b, ln_g, ln_b, hy_w_in, hy_conv_w, hy_conv_b, hy_f_w1, hy_f_b1, hy_f_w2, hy_f_b2, hy_f_w3, hy_f_freq, hy_f_bias, hy_w_out, at_w_qkv, at_q_gain, at_k_gain, at_w_out, cm_w_in, cm_ln_g, cm_ln_b, cm_w_s, cm_b_s, cm_w_out, moe_router, moe_w_gate, moe_w_up, moe_w_down):
    B, L, D = x.shape
    Lc = ctx.shape[1]
    x_lat, x_ctx = x, ctx
    for i in range(DEPTH):
        kind, slot = LAYER_KINDS[i], KIND_SLOT[i]
        ctx_read = i <= LAST_CTX_LAYER
        ctx_update = i < LAST_CTX_LAYER
        sh1, sc1, g1, sh2, sc2, g2 = _adaln(c[:, None, :], mod_w[i], mod_b[i])
        if ctx_read:
            cmod = [jnp.broadcast_to(m[None, None, :], (B, 1, D)) for m in _adaln(c_ctx, mod_w[i], mod_b[i])]
        lg0, lb0, lg1, lb1 = ln_g[i, 0][None], ln_b[i, 0][None], ln_g[i, 1][None], ln_b[i, 1][None]

        if kind == 0:
            w_in = hy_w_in[slot].astype(BF16)
            w_out = hy_w_out[slot].astype(BF16)
            hy = (hy_conv_w[slot], hy_conv_b[slot], hy_f_w1[slot], hy_f_b1[slot], hy_f_w2[slot], hy_f_b2[slot],
                  hy_f_w3[slot], hy_f_freq[slot], hy_f_bias[slot])
            z = _hyena_core(mod_matmul(x_lat, sh1, sc1, w_in), *hy)
            x_lat = matmul_postnorm(z, w_out, x_lat, g1, lg0, lb0)
            if ctx_update:
                zc = _hyena_core(mod_matmul(x_ctx, cmod[0], cmod[1], w_in), *hy)
                x_ctx = matmul_postnorm(zc, w_out, x_ctx, cmod[2], lg0, lb0)
        elif kind == 1:
            w_qkv = at_w_qkv[slot].astype(BF16)
            w_out = at_w_out[slot].astype(BF16).reshape(N_Q_HEADS, HEAD_DIM, D)
            cos, sin = _axial_rope_tables(L)
            cos_t = jnp.tile(cos, (1, 128 // cos.shape[1]))
            sin_t = jnp.tile(jnp.concatenate([-sin, sin], axis=-1), (1, 128 // HEAD_DIM))
            qkv = mod_matmul(x_lat, sh1, sc1, w_qkv)
            q, k, v = qkv_prep(qkv, at_q_gain[slot], at_k_gain[slot], N_Q_HEADS, N_KV_HEADS, (cos_t, sin_t))
            kvc = mod_matmul(x_ctx, cmod[0], cmod[1], w_qkv[:, Q_COLS:])
            ck, cv = qkv_prep(kvc, None, at_k_gain[slot], 0, N_KV_HEADS, None)
            o = attention(q, ck, cv, k, v)
            x_lat = heads_matmul_postnorm(o, w_out, x_lat, g1, lg0, lb0)
        else:
            w_in = cm_w_in[slot].astype(BF16)
            w_out = cm_w_out[slot].astype(BF16)
            zz = mod_matmul(x_lat, sh1, sc1, w_in, gelu=True)
            x_lat = gmlp_tail(zz, cm_ln_g[slot][None], cm_ln_b[slot][None], cm_w_s[slot].astype(BF16),
                              cm_b_s[slot].T, w_out, x_lat, g1, lg0, lb0)

        moe = (moe_router[i].T.astype(BF16), moe_w_gate, moe_w_up, moe_w_down, i)
        x_lat = postnorm(_expert_choice_ffn(x_lat, sh2, sc2, *moe), x_lat, g2, lg1, lb1)
        if ctx_update:
            x_ctx = postnorm(_expert_choice_ffn(x_ctx, cmod[3], cmod[4], *moe), x_ctx, cmod[5], lg1, lb1)
    return x_lat
```

```python
import math
from functools import partial

import jax
import jax.numpy as jnp
from jax import lax
from jax.experimental import pallas as pl
from jax.experimental.pallas import tpu as pltpu

F32 = jnp.float32
BF16 = jnp.bfloat16

D_MODEL = 1024
DEPTH = 4
GRID_W = 64
N_MIXERS = 3
LAYER_KINDS = tuple(i % N_MIXERS for i in range(DEPTH))
KIND_SLOT = tuple(LAYER_KINDS[:i].count(LAYER_KINDS[i]) for i in range(DEPTH))
LAST_CTX_LAYER = max([i for i, k in enumerate(LAYER_KINDS) if k == 1], default=-1)

HY_ORDER = 2
HY_BANDS = 16
HY_DECAY_TARGET = 1e-2
HY_FAST_DECAY_PCT = 0.3
HY_SLOW_DECAY_PCT = 1.5
HY_MIN_DECAY = math.log(HY_DECAY_TARGET) / HY_SLOW_DECAY_PCT
HY_MAX_DECAY = math.log(HY_DECAY_TARGET) / HY_FAST_DECAY_PCT

HEAD_DIM = 64
V_EXT = 128
N_Q_HEADS = D_MODEL // HEAD_DIM
N_KV_HEADS = 4
GQA_GROUP = N_Q_HEADS // N_KV_HEADS
ROPE_THETA = 10000.0
ATTN_SCALE = HEAD_DIM ** -0.5
Q_COLS = N_Q_HEADS * HEAD_DIM

CHUNK = 128
GMLP_HALF = 2 * D_MODEL
GMLP_GROUPS = 16
GMLP_GROUP_W = GMLP_HALF // GMLP_GROUPS

N_EXPERTS = 16
CAPACITY_FACTOR = 2

RMS_EPS = 1e-6
LN_EPS = 1e-5
DEEPNORM_ALPHA = (2 * DEPTH) ** 0.25

VMEM_LIMIT_BYTES = 56 * 1024 * 1024
OUT_TILE_BYTES = 4 * 1024 * 1024


def _row_tile(rows, n_cols):
    tm = 512
    while tm > 8 and (tm * n_cols * 4 > OUT_TILE_BYTES or rows % tm):
        tm //= 2
    return tm


def _params(sem):
    return pltpu.CompilerParams(dimension_semantics=sem, vmem_limit_bytes=VMEM_LIMIT_BYTES)


def _gelu_erf(x):
    return 0.5 * x * (1.0 + lax.erf(x * (2.0 ** -0.5)))


def _mod_matmul_kernel(x_ref, sh_ref, sc_ref, w_ref, o_ref):
    h = x_ref[0] * (1.0 + sc_ref[0]) + sh_ref[0]
    o_ref[0] = jnp.dot(h.astype(BF16), w_ref[...], preferred_element_type=F32)


def mod_matmul(x, shift, scale, w):
    B, L, K = x.shape
    N = w.shape[1]
    tm = _row_tile(L, N)
    return pl.pallas_call(
        _mod_matmul_kernel,
        grid=(B, L // tm),
        in_specs=[
            pl.BlockSpec((1, tm, K), lambda b, i: (b, i, 0)),
            pl.BlockSpec((1, 1, K), lambda b, i: (b, 0, 0)),
            pl.BlockSpec((1, 1, K), lambda b, i: (b, 0, 0)),
            pl.BlockSpec((K, N), lambda b, i: (0, 0)),
        ],
        out_specs=pl.BlockSpec((1, tm, N), lambda b, i: (b, i, 0)),
        out_shape=jax.ShapeDtypeStruct((B, L, N), F32),
        compiler_params=_params(("parallel", "parallel")),
        name="mod_matmul",
    )(x, shift, scale, w)


def _layer_norm_rows(t, g, b):
    mu = jnp.mean(t, axis=-1, keepdims=True)
    d = t - mu
    var = jnp.mean(d * d, axis=-1, keepdims=True)
    return d * lax.rsqrt(var + LN_EPS) * g + b


def _matmul_postnorm_kernel(z_ref, w_ref, x_ref, gate_ref, g_ref, b_ref, o_ref):
    y = jnp.dot(z_ref[0].astype(BF16), w_ref[...], preferred_element_type=F32)
    t = DEEPNORM_ALPHA * x_ref[0] + gate_ref[0] * y
    o_ref[0] = _layer_norm_rows(t, g_ref[...], b_ref[...])


def matmul_postnorm(z, w, x, gate, g, b):
    B, L, K = z.shape
    D = w.shape[1]
    tm = _row_tile(L, max(K, D))
    return pl.pallas_call(
        _matmul_postnorm_kernel,
        grid=(B, L // tm),
        in_specs=[
            pl.BlockSpec((1, tm, K), lambda bi, i: (bi, i, 0)),
            pl.BlockSpec((K, D), lambda bi, i: (0, 0)),
            pl.BlockSpec((1, tm, D), lambda bi, i: (bi, i, 0)),
            pl.BlockSpec((1, 1, D), lambda bi, i: (bi, 0, 0)),
            pl.BlockSpec((1, D), lambda bi, i: (0, 0)),
            pl.BlockSpec((1, D), lambda bi, i: (0, 0)),
        ],
        out_specs=pl.BlockSpec((1, tm, D), lambda bi, i: (bi, i, 0)),
        out_shape=jax.ShapeDtypeStruct((B, L, D), F32),
        compiler_params=_params(("parallel", "parallel")),
        name="matmul_postnorm",
    )(z, w, x, gate, g, b)


def _postnorm_kernel(y_ref, x_ref, gate_ref, g_ref, b_ref, o_ref):
    t = DEEPNORM_ALPHA * x_ref[0] + gate_ref[0] * y_ref[0]
    o_ref[0] = _layer_norm_rows(t, g_ref[...], b_ref[...])


def postnorm(y, x, gate, g, b):
    B, L, D = x.shape
    tm = _row_tile(L, D)
    return pl.pallas_call(
        _postnorm_kernel,
        grid=(B, L // tm),
        in_specs=[
            pl.BlockSpec((1, tm, D), lambda bi, i: (bi, i, 0)),
            pl.BlockSpec((1, tm, D), lambda bi, i: (bi, i, 0)),
            pl.BlockSpec((1, 1, D), lambda bi, i: (bi, 0, 0)),
            pl.BlockSpec((1, D), lambda bi, i: (0, 0)),
            pl.BlockSpec((1, D), lambda bi, i: (0, 0)),
        ],
        out_specs=pl.BlockSpec((1, tm, D), lambda bi, i: (bi, i, 0)),
        out_shape=jax.ShapeDtypeStruct((B, L, D), F32),
        compiler_params=_params(("parallel", "parallel")),
        name="postnorm",
    )(y, x, gate, g, b)


def _gmlp_block_kernel(x_ref, sh_ref, sc_ref, win_ref, lng_ref, lnb_ref, ws_ref, bs_ref, w_ref, gate_ref, g_ref, b_ref,
                       o_ref):
    h = (x_ref[0] * (1.0 + sc_ref[0]) + sh_ref[0]).astype(BF16)
    z = _gelu_erf(jnp.dot(h, win_ref[...], preferred_element_type=F32))
    tm, half = z.shape[0], z.shape[1] // 2
    u, v = z[:, :half], z[:, half:]
    vn = _layer_norm_rows(v, lng_ref[...], lnb_ref[...]).astype(BF16)
    n_grp = ws_ref.shape[0]
    gw = half // n_grp
    chunks = []
    for c in range(tm // CHUNK):
        cols = []
        for g in range(n_grp):
            blk = vn[c * CHUNK:(c + 1) * CHUNK, g * gw:(g + 1) * gw]
            cols.append(jnp.dot(ws_ref[g], blk, preferred_element_type=F32) + bs_ref[:, g:g + 1])
        chunks.append(jnp.concatenate(cols, axis=1))
    sv = jnp.concatenate(chunks, axis=0)
    y = jnp.dot((u * sv).astype(BF16), w_ref[...], preferred_element_type=F32)
    t = DEEPNORM_ALPHA * x_ref[0] + gate_ref[0] * y
    o_ref[0] = _layer_norm_rows(t, g_ref[...], b_ref[...])


def gmlp_block(x, shift, scale, w_in, ln_g, ln_b, w_s, b_s_t, w_out, gate, g, b, tm=256):
    B, L, D = x.shape
    H = w_out.shape[0]
    const = lambda shape: pl.BlockSpec(shape, lambda bi, i: (0,) * len(shape))
    vec = pl.BlockSpec((1, 1, D), lambda bi, i: (bi, 0, 0))
    return pl.pallas_call(
        _gmlp_block_kernel,
        grid=(B, L // tm),
        in_specs=[
            pl.BlockSpec((1, tm, D), lambda bi, i: (bi, i, 0)), vec, vec, const((D, 2 * H)),
            const((1, H)), const((1, H)), const(w_s.shape), const(b_s_t.shape), const((H, D)),
            vec, const((1, D)), const((1, D)),
        ],
        out_specs=pl.BlockSpec((1, tm, D), lambda bi, i: (bi, i, 0)),
        out_shape=jax.ShapeDtypeStruct((B, L, D), F32),
        compiler_params=_params(("parallel", "parallel")),
        name="gmlp_block",
    )(x, shift, scale, w_in, ln_g, ln_b, w_s, b_s_t, w_out, gate, g, b)


def _head_rms_norm(x, ones_bd, gain):
    sq = x * x
    sq_hi = sq.astype(BF16)
    sq_lo = (sq - sq_hi.astype(F32)).astype(BF16)
    ss = jnp.dot(sq_hi, ones_bd, preferred_element_type=F32) + jnp.dot(sq_lo, ones_bd, preferred_element_type=F32)
    return x * lax.rsqrt(ss * (1.0 / HEAD_DIM) + RMS_EPS) * gain


def _rope(x, cos_t, sin_t):
    w = x.shape[1]
    lane = lax.broadcasted_iota(jnp.int32, x.shape, 1)
    half = HEAD_DIM // 2
    partner = jnp.where((lane & (HEAD_DIM - 1)) < half, pltpu.roll(x, w - half, 1), pltpu.roll(x, half, 1))
    reps = w // cos_t.shape[1]
    return x * jnp.tile(cos_t, (1, reps)) + partner * jnp.tile(sin_t, (1, reps))


def _qkv_prep_kernel(*refs, n_q, rope):
    refs = list(refs)
    x_ref = refs.pop(0)
    cos_ref, sin_ref = (refs.pop(0), refs.pop(0)) if rope else (None, None)
    if n_q:
        qg_ref, qbd_ref = refs.pop(0), refs.pop(0)
    kg_ref, kbd_ref = refs.pop(0), refs.pop(0)
    if n_q:
        q_out = refs.pop(0)
    k_out, v_out = refs
    x = x_ref[0]
    n_kv = k_out.shape[1]
    qc, kc = n_q * HEAD_DIM, n_kv * HEAD_DIM
    if n_q:
        q = _head_rms_norm(x[:, :qc], qbd_ref[...], qg_ref[...])
        if rope:
            q = _rope(q, cos_ref[...], sin_ref[...])
        q = q * ATTN_SCALE
        for h in range(n_q):
            q_out[0, h] = q[:, h * HEAD_DIM:(h + 1) * HEAD_DIM].astype(BF16)
    k = _head_rms_norm(x[:, qc:qc + kc], kbd_ref[...], kg_ref[...])
    if rope:
        k = _rope(k, cos_ref[...], sin_ref[...])
    lane = lax.broadcasted_iota(jnp.int32, (x.shape[0], V_EXT), 1)
    for h in range(n_kv):
        k_out[0, h] = k[:, h * HEAD_DIM:(h + 1) * HEAD_DIM].astype(BF16)
        c0 = qc + kc + h * HEAD_DIM
        base = c0 // V_EXT * V_EXT
        vv = x[:, base:base + V_EXT]
        if c0 != base:
            vv = pltpu.roll(vv, V_EXT - (c0 - base), 1)
        v_out[0, h] = jnp.where(lane < HEAD_DIM, vv, (lane == HEAD_DIM).astype(F32)).astype(BF16)


def qkv_prep(x, q_gain, k_gain, n_q, n_kv, rope_tables, tm=256):
    B, L, C = x.shape
    hd = HEAD_DIM
    tm = min(tm, L)

    def ones_bd(n):
        r = jnp.arange(n * hd) // hd
        return (r[:, None] == r[None, :]).astype(BF16)

    const = lambda shape: pl.BlockSpec(shape, lambda b, i: (0,) * len(shape))
    args, specs = [x], [pl.BlockSpec((1, tm, C), lambda b, i: (b, i, 0))]
    if rope_tables is not None:
        args += list(rope_tables)
        specs += [pl.BlockSpec((tm, 128), lambda b, i: (i, 0))] * 2
    if n_q:
        args += [jnp.tile(q_gain, n_q)[None], ones_bd(n_q)]
        specs += [const((1, n_q * hd)), const((n_q * hd, n_q * hd))]
    args += [jnp.tile(k_gain, n_kv)[None], ones_bd(n_kv)]
    specs += [const((1, n_kv * hd)), const((n_kv * hd, n_kv * hd))]
    head_spec = lambda n, w=hd: pl.BlockSpec((1, n, tm, w), lambda b, i: (b, 0, i, 0))
    head_shape = lambda n, w=hd: jax.ShapeDtypeStruct((B, n, L, w), BF16)
    out_specs = ([head_spec(n_q)] if n_q else []) + [head_spec(n_kv), head_spec(n_kv, V_EXT)]
    out_shape = ([head_shape(n_q)] if n_q else []) + [head_shape(n_kv), head_shape(n_kv, V_EXT)]
    return pl.pallas_call(
        partial(_qkv_prep_kernel, n_q=n_q, rope=rope_tables is not None),
        grid=(B, L // tm),
        in_specs=specs,
        out_specs=out_specs,
        out_shape=out_shape,
        compiler_params=_params(("parallel", "parallel")),
        name="qkv_prep",
    )(*args)


def _nt_dot(a, b):
    return lax.dot_general(a, b, (((1,), (1,)), ((), ())), preferred_element_type=F32)


def _attention_kernel(q_ref, ck_ref, cv_ref, k_ref, v_ref, o_ref):
    g, tq, hd = q_ref.shape[1], q_ref.shape[2], q_ref.shape[3]
    q = q_ref[0].reshape(g * tq, hd)
    s_c = _nt_dot(q, ck_ref[0, 0])
    s_l = _nt_dot(q, k_ref[0, 0])
    m = jnp.maximum(jnp.max(s_c, axis=-1, keepdims=True), jnp.max(s_l, axis=-1, keepdims=True))
    p_c = jnp.exp(s_c - m).astype(BF16)
    p_l = jnp.exp(s_l - m).astype(BF16)
    o = (jnp.dot(p_c, cv_ref[0, 0], preferred_element_type=F32)
         + jnp.dot(p_l, v_ref[0, 0], preferred_element_type=F32))
    o = o[:, :hd] / o[:, hd:hd + 1]
    o_ref[0] = o.astype(BF16).reshape(g, tq, hd)


def attention(q, ck, cv, k, v, tq=128):
    B, H, L, hd = q.shape
    KV, Lc = ck.shape[1], ck.shape[2]
    G = H // KV
    kv_spec = lambda n, w=hd: pl.BlockSpec((1, 1, n, w), lambda b, kv, i: (b, kv, 0, 0))
    q_spec = pl.BlockSpec((1, G, tq, hd), lambda b, kv, i: (b, kv, i, 0))
    return pl.pallas_call(
        _attention_kernel,
        grid=(B, KV, L // tq),
        in_specs=[q_spec, kv_spec(Lc), kv_spec(Lc, V_EXT), kv_spec(L), kv_spec(L, V_EXT)],
        out_specs=q_spec,
        out_shape=jax.ShapeDtypeStruct((B, H, L, hd), BF16),
        compiler_params=_params(("parallel", "parallel", "parallel")),
        name="gqa_attention",
    )(q, ck, cv, k, v)


def _heads_matmul_postnorm_kernel(o_ref, w_ref, x_ref, gate_ref, g_ref, b_ref, out_ref):
    y = jnp.dot(o_ref[0, 0], w_ref[0], preferred_element_type=F32)
    for h in range(1, o_ref.shape[1]):
        y += jnp.dot(o_ref[0, h], w_ref[h], preferred_element_type=F32)
    t = DEEPNORM_ALPHA * x_ref[0] + gate_ref[0] * y
    out_ref[0] = _layer_norm_rows(t, g_ref[...], b_ref[...])


def heads_matmul_postnorm(o, w, x, gate, g, b, tm=512):
    B, H, L, hd = o.shape
    D = w.shape[2]
    return pl.pallas_call(
        _heads_matmul_postnorm_kernel,
        grid=(B, L // tm),
        in_specs=[
            pl.BlockSpec((1, H, tm, hd), lambda bi, i: (bi, 0, i, 0)),
            pl.BlockSpec((H, hd, D), lambda bi, i: (0, 0, 0)),
            pl.BlockSpec((1, tm, D), lambda bi, i: (bi, i, 0)),
            pl.BlockSpec((1, 1, D), lambda bi, i: (bi, 0, 0)),
            pl.BlockSpec((1, D), lambda bi, i: (0, 0)),
            pl.BlockSpec((1, D), lambda bi, i: (0, 0)),
        ],
        out_specs=pl.BlockSpec((1, tm, D), lambda bi, i: (bi, i, 0)),
        out_shape=jax.ShapeDtypeStruct((B, L, D), F32),
        compiler_params=_params(("parallel", "parallel")),
        name="heads_matmul_postnorm",
    )(o, w, x, gate, g, b)


def _router_kernel(x_ref, sh_ref, sc_ref, w_ref, o_ref):
    h = (x_ref[0] * (1.0 + sc_ref[0]) + sh_ref[0]).astype(BF16)
    logits = _nt_dot(w_ref[...], h)
    e = jnp.exp(logits - jnp.max(logits, axis=0, keepdims=True))
    o_ref[0] = e / jnp.sum(e, axis=0, keepdims=True)


def router(x, shift, scale, w_t):
    B, n, D = x.shape
    E = w_t.shape[0]
    tm = min(n, 512)
    vec = pl.BlockSpec((1, 1, D), lambda b, i: (b, 0, 0))
    return pl.pallas_call(
        _router_kernel,
        grid=(B, n // tm),
        in_specs=[pl.BlockSpec((1, tm, D), lambda b, i: (b, i, 0)), vec, vec, pl.BlockSpec((E, D), lambda b, i: (0, 0))],
        out_specs=pl.BlockSpec((1, E, tm), lambda b, i: (b, 0, i)),
        out_shape=jax.ShapeDtypeStruct((B, E, n), F32),
        compiler_params=_params(("parallel", "parallel")),
        name="moe_router",
    )(x, shift, scale, w_t)


ROUTE_UNROLL = 8


def _gather_rows_kernel(idx_ref, x_ref, sh_ref, sc_ref, o_ref, buf):
    cap = buf.shape[0]
    base = (pl.program_id(0) * pl.num_programs(1) + pl.program_id(1)) * cap

    def body(s, carry):
        for u in range(ROUTE_UNROLL):
            r = s * ROUTE_UNROLL + u
            buf[pl.ds(r, 1), :] = x_ref[pl.ds(idx_ref[base + r], 1), :]
        return carry

    lax.fori_loop(0, cap // ROUTE_UNROLL, body, 0)
    o_ref[...] = (buf[...] * (1.0 + sc_ref[...]) + sh_ref[...]).astype(BF16)


def gather_rows(x, shift, scale, idx):
    B, n, D = x.shape
    _, E, cap = idx.shape
    vec = pl.BlockSpec((None, 1, D), lambda b, e, idx: (b, 0, 0))
    return pl.pallas_call(
        _gather_rows_kernel,
        grid_spec=pltpu.PrefetchScalarGridSpec(
            num_scalar_prefetch=1,
            grid=(B, E),
            in_specs=[pl.BlockSpec((None, n, D), lambda b, e, idx: (b, 0, 0)), vec, vec],
            out_specs=pl.BlockSpec((None, None, cap, D), lambda b, e, idx: (e, b, 0, 0)),
            scratch_shapes=[pltpu.VMEM((cap, D), F32)],
        ),
        out_shape=jax.ShapeDtypeStruct((E, B, cap, D), BF16),
        compiler_params=_params(("parallel", "arbitrary")),
        name="moe_gather_rows",
    )(idx.reshape(-1), x, shift, scale)


def _scatter_add_kernel(idx_ref, y_ref, o_ref):
    cap = y_ref.shape[0]
    base = (pl.program_id(0) * pl.num_programs(1) + pl.program_id(1)) * cap

    @pl.when(pl.program_id(1) == 0)
    def _():
        o_ref[...] = jnp.zeros(o_ref.shape, o_ref.dtype)

    def body(s, carry):
        rows = [idx_ref[base + s * ROUTE_UNROLL + u] for u in range(ROUTE_UNROLL)]
        sums = [o_ref[pl.ds(rows[u], 1), :] + y_ref[pl.ds(s * ROUTE_UNROLL + u, 1), :] for u in range(ROUTE_UNROLL)]
        for u in range(ROUTE_UNROLL):
            o_ref[pl.ds(rows[u], 1), :] = sums[u]
        return carry

    lax.fori_loop(0, cap // ROUTE_UNROLL, body, 0)


def scatter_add_rows(y, idx, n):
    E, R, D = y.shape
    B, _, cap = idx.shape
    return pl.pallas_call(
        _scatter_add_kernel,
        grid_spec=pltpu.PrefetchScalarGridSpec(
            num_scalar_prefetch=1,
            grid=(B, E),
            in_specs=[pl.BlockSpec((None, cap, D), lambda b, e, idx: (e, b, 0))],
            out_specs=pl.BlockSpec((None, n, D), lambda b, e, idx: (b, 0, 0)),
        ),
        out_shape=jax.ShapeDtypeStruct((B, n, D), F32),
        compiler_params=_params(("parallel", "arbitrary")),
        name="moe_scatter_add_rows",
    )(idx.reshape(-1), y)


def _expert_ffn_kernel(x_ref, wg_ref, wu_ref, wd_ref, gate_ref, o_ref):
    f = pl.program_id(1)
    x = x_ref[0]
    a = jnp.dot(x, wg_ref[0].astype(BF16), preferred_element_type=F32)
    u = jnp.dot(x, wu_ref[0].astype(BF16), preferred_element_type=F32)
    h = (a * jax.nn.sigmoid(a) * u).astype(BF16)
    y = jnp.dot(h, wd_ref[0].astype(BF16), preferred_element_type=F32)

    @pl.when(f == 0)
    def _():
        o_ref[0] = y

    @pl.when(f > 0)
    def _():
        o_ref[0] += y

    @pl.when(f == pl.num_programs(1) - 1)
    def _():
        o_ref[0] *= gate_ref[0]


def expert_ffn(xg, w_gate, w_up, w_down, layer, gate, tf=512):
    E, R, D = xg.shape
    F = w_gate.shape[3]
    return pl.pallas_call(
        _expert_ffn_kernel,
        grid=(E, F // tf),
        in_specs=[
            pl.BlockSpec((1, R, D), lambda e, f: (e, 0, 0)),
            pl.BlockSpec((None, 1, D, tf), lambda e, f: (layer, e, 0, f)),
            pl.BlockSpec((None, 1, D, tf), lambda e, f: (layer, e, 0, f)),
            pl.BlockSpec((None, 1, tf, D), lambda e, f: (layer, e, f, 0)),
            pl.BlockSpec((1, R, 1), lambda e, f: (e, 0, 0)),
        ],
        out_specs=pl.BlockSpec((1, R, D), lambda e, f: (e, 0, 0)),
        out_shape=jax.ShapeDtypeStruct((E, R, D), F32),
        compiler_params=_params(("parallel", "arbitrary")),
        name="expert_ffn",
    )(xg, w_gate, w_up, w_down, gate)


def _short_conv_kernel(p_ref, w_ref, b_ref, o_ref):
    q = p_ref[0]
    L = q.shape[0]
    row = lax.broadcasted_iota(jnp.int32, q.shape, 0)
    prev = jnp.where(row == 0, 0.0, pltpu.roll(q, 1, 0))
    nxt = jnp.where(row == L - 1, 0.0, pltpu.roll(q, L - 1, 0))
    w = w_ref[...]
    o_ref[0, 0] = prev * w[0:1] + q * w[1:2] + nxt * w[2:3] + b_ref[...]


def short_conv(p, conv_w, conv_b, tc=256):
    B, L, D3 = p.shape
    D = D3 // 3
    nj = D // tc
    return pl.pallas_call(
        _short_conv_kernel,
        grid=(3, B, nj),
        in_specs=[
            pl.BlockSpec((1, L, tc), lambda s, b, j: (b, 0, s * nj + j)),
            pl.BlockSpec((3, tc), lambda s, b, j: (0, s * nj + j)),
            pl.BlockSpec((1, tc), lambda s, b, j: (0, s * nj + j)),
        ],
        out_specs=pl.BlockSpec((1, 1, L, tc), lambda s, b, j: (s, b, 0, j)),
        out_shape=jax.ShapeDtypeStruct((3, B, L, D), F32),
        compiler_params=_params(("parallel", "parallel", "parallel")),
        name="hyena_short_conv",
    )(p, conv_w, conv_b[None])


FFT_GROUP = 128
SUBLANES = 8
FFT_DATA_PASSES = 1
FFT_FILTER_PASSES = 3


def _split_hi_lo(m):
    hi = m.astype(BF16)
    return hi, (m - hi.astype(F32)).astype(BF16)


def _dot_split(m_hi, m_lo, x, passes):
    x_hi = x.astype(BF16)
    acc = jnp.dot(m_hi, x_hi, preferred_element_type=F32)
    if passes >= 2:
        x_lo = (x - x_hi.astype(F32)).astype(BF16)
        acc += jnp.dot(m_hi, x_lo, preferred_element_type=F32)
    if passes >= 3:
        acc += jnp.dot(m_lo, x_hi, preferred_element_type=F32)
    return acc


def _cos_sin(num, den):
    ang = (2.0 * math.pi / den) * (num % den).astype(F32)
    return jnp.cos(ang), jnp.sin(ang)


def _outer_tables(N1, n_in):
    k1 = jnp.arange(N1, dtype=jnp.int32)
    c, s = _cos_sin(k1[:, None] * jnp.arange(n_in, dtype=jnp.int32)[None, :], N1)
    fwd = jnp.concatenate([jnp.concatenate([c, s], 1), jnp.concatenate([-s, c], 1)], 0)
    cf, sf = _cos_sin(k1[:, None] * k1[None, :], N1)
    fwd_real = jnp.concatenate([cf, -sf], 0)
    ci, si = c.T, s.T
    inv = jnp.concatenate([jnp.concatenate([ci, -si], 1), jnp.concatenate([si, ci], 1)], 0) / (N1 * FFT_GROUP)
    return fwd, fwd_real, inv


def _group_tables(N1):
    G = FFT_GROUP
    k = jnp.arange(N1, dtype=jnp.int32)[:, None, None] + N1 * jnp.arange(G, dtype=jnp.int32)[None, :, None]
    gr, gs = _cos_sin(k * jnp.arange(G, dtype=jnp.int32)[None, None, :], N1 * G)
    gi = -gs
    fwd = jnp.concatenate([jnp.concatenate([gr, -gi], 2), jnp.concatenate([gi, gr], 2)], 1)
    return fwd, jnp.swapaxes(fwd, 1, 2)


def _tile_expand(m):
    r, c = m.shape
    eye = jnp.eye(SUBLANES, dtype=m.dtype)
    return (m[:, None, :, None] * eye[None, :, None, :]).reshape(r * SUBLANES, c * SUBLANES)


def _outer_fwd_kernel(z_ref, m_hi, m_lo, o_ref, *, passes):
    parts, n_in, sub, d = z_ref.shape
    x = z_ref[...].reshape(parts * n_in * sub, d)
    r = _dot_split(m_hi[...], m_lo[...], x, passes)
    o_ref[0] = r.reshape(2, r.shape[0] // (2 * sub), sub, d)


def outer_fwd(z, m, parts, first, n_seq):
    _, n_in, G, D = z.shape
    N1 = m.shape[0] // 2
    m_hi, m_lo = _split_hi_lo(_tile_expand(m))
    first_blk = first // parts
    return pl.pallas_call(
        partial(_outer_fwd_kernel, passes=FFT_DATA_PASSES if parts == 2 else FFT_FILTER_PASSES),
        grid=(n_seq, G // SUBLANES),
        in_specs=[
            pl.BlockSpec((parts, n_in, SUBLANES, D), lambda p, j: (first_blk + p, 0, j, 0)),
            pl.BlockSpec(m_hi.shape, lambda p, j: (0, 0)),
            pl.BlockSpec(m_hi.shape, lambda p, j: (0, 0)),
        ],
        out_specs=pl.BlockSpec((1, 2, N1, SUBLANES, D), lambda p, j: (p, 0, 0, j, 0)),
        out_shape=jax.ShapeDtypeStruct((n_seq, 2, N1, G, D), F32),
        compiler_params=_params(("parallel", "parallel")),
        name="hyena_dft_outer_fwd",
    )(z, m_hi, m_lo)


def _group_spectrum_kernel(a_ref, g_hi, g_lo, o_ref):
    a = a_ref[0]
    x = _dot_split(g_hi[0], g_lo[0], a.reshape(a.shape[0] * a.shape[1], a.shape[2]), FFT_FILTER_PASSES)
    o_ref[...] = x.reshape(2, x.shape[0] // 2, x.shape[1])


def group_spectrum(a, g):
    _, parts, rows, D = a.shape
    n_grp, Rm2 = g.shape[0], g.shape[1]
    Rin = rows // n_grp
    g_hi, g_lo = _split_hi_lo(g)
    return pl.pallas_call(
        _group_spectrum_kernel,
        grid=(n_grp,),
        in_specs=[
            pl.BlockSpec((1, parts, Rin, D), lambda k: (0, 0, k, 0)),
            pl.BlockSpec((1,) + g.shape[1:], lambda k: (k, 0, 0)),
            pl.BlockSpec((1,) + g.shape[1:], lambda k: (k, 0, 0)),
        ],
        out_specs=pl.BlockSpec((2, Rm2 // 2, D), lambda k: (0, k, 0)),
        out_shape=jax.ShapeDtypeStruct((2, n_grp * Rm2 // 2, D), F32),
        compiler_params=_params(("parallel",)),
        name="hyena_filter_spectrum",
    )(a, g_hi, g_lo)


def _group_conv_kernel(a_ref, kf_ref, g_hi, g_lo, h_hi, h_lo, o_ref):
    n_grp = g_hi.shape[0]
    r_in, r_mid, r_out = a_ref.shape[2] // n_grp, kf_ref.shape[1] // n_grp, o_ref.shape[2] // n_grp
    for j in range(n_grp):
        a = jnp.concatenate([a_ref[0, 0, j * r_in:(j + 1) * r_in], a_ref[0, 1, j * r_in:(j + 1) * r_in]], axis=0)
        x = _dot_split(g_hi[j], g_lo[j], a, FFT_DATA_PASSES)
        xr, xi = x[:r_mid], x[r_mid:]
        kr, ki = kf_ref[0, j * r_mid:(j + 1) * r_mid], kf_ref[1, j * r_mid:(j + 1) * r_mid]
        y = jnp.concatenate([xr * kr - xi * ki, xr * ki + xi * kr], axis=0)
        p = _dot_split(h_hi[j], h_lo[j], y, FFT_DATA_PASSES)
        o_ref[0, 0, j * r_out:(j + 1) * r_out] = p[:r_out]
        o_ref[0, 1, j * r_out:(j + 1) * r_out] = p[r_out:]


def group_conv(a, kf, g, h):
    P, _, rows, D = a.shape
    n_grp, Rm2, Rout2 = g.shape[0], g.shape[1], h.shape[1]
    Rin = rows // n_grp
    gs = 2 if n_grp % 2 == 0 else 1
    g_hi, g_lo = _split_hi_lo(g)
    h_hi, h_lo = _split_hi_lo(h)
    gspec = pl.BlockSpec((gs,) + g.shape[1:], lambda k, p: (k, 0, 0))
    hspec = pl.BlockSpec((gs,) + h.shape[1:], lambda k, p: (k, 0, 0))
    return pl.pallas_call(
        _group_conv_kernel,
        grid=(n_grp // gs, P),
        in_specs=[
            pl.BlockSpec((1, 2, gs * Rin, D), lambda k, p: (p, 0, k, 0)),
            pl.BlockSpec((2, gs * Rm2 // 2, D), lambda k, p: (0, k, 0)),
            gspec, gspec, hspec, hspec,
        ],
        out_specs=pl.BlockSpec((1, 2, gs * Rout2 // 2, D), lambda k, p: (p, 0, k, 0)),
        out_shape=jax.ShapeDtypeStruct((P, 2, n_grp * Rout2 // 2, D), F32),
        compiler_params=_params(("parallel", "parallel")),
        name="hyena_group_conv",
    )(a, kf, g_hi, g_lo, h_hi, h_lo)


def _outer_inv_kernel(p_ref, m_hi, m_lo, gate_ref, u_ref, bias_ref, o_ref):
    _, _, n1, sub, d = p_ref.shape
    y = _dot_split(m_hi[...], m_lo[...], p_ref[0].reshape(2 * n1 * sub, d), FFT_DATA_PASSES)
    y = y.reshape(2, y.shape[0] // (2 * sub), sub, d)
    o_ref[...] = gate_ref[...] * (y + u_ref[...] * bias_ref[...])


def outer_inv(pm, m, gate, gate_first, u, u_first, bias):
    P, _, N1, G, D = pm.shape
    n_out = m.shape[0] // 2
    m_hi, m_lo = _split_hi_lo(_tile_expand(m))
    seq = lambda first: pl.BlockSpec((2, n_out, SUBLANES, D), lambda p, j: (first // 2 + p, 0, j, 0))
    return pl.pallas_call(
        _outer_inv_kernel,
        grid=(P, G // SUBLANES),
        in_specs=[
            pl.BlockSpec((1, 2, N1, SUBLANES, D), lambda p, j: (p, 0, 0, j, 0)),
            pl.BlockSpec(m_hi.shape, lambda p, j: (0, 0)),
            pl.BlockSpec(m_hi.shape, lambda p, j: (0, 0)),
            seq(gate_first), seq(u_first),
            pl.BlockSpec((1, D), lambda p, j: (0, 0)),
        ],
        out_specs=seq(0),
        out_shape=jax.ShapeDtypeStruct((2 * P, n_out, G, D), F32),
        compiler_params=_params(("parallel", "parallel")),
        name="hyena_dft_outer_inv",
    )(pm, m_hi, m_lo, gate, u, bias)


def _hyena_long_convs(vx, k2, f_bias):
    _, B, L, D = vx.shape
    N = 2 * L
    G = FFT_GROUP
    if N // G >= 16:
        N1, Nh = N // G, N // G // 2
        fwd, fwd_real, inv = _outer_tables(N1, Nh)
        g_fwd, g_inv = _group_tables(N1)
        vx4 = vx.reshape(3 * B, Nh, G, D)
        k4 = k2.reshape(HY_ORDER, N1, G, D)
        z, z_first = vx4, 0
        for n in range(HY_ORDER):
            kf = group_spectrum(outer_fwd(k4, fwd_real, 1, n, 1).reshape(1, 2, N, D), g_fwd)
            a = outer_fwd(z, fwd, 2, z_first, B // 2).reshape(B // 2, 2, N, D)
            pm = group_conv(a, kf, g_fwd, g_inv).reshape(B // 2, 2, N1, G, D)
            z = outer_inv(pm, inv, vx4, (n + 1) * B, z, z_first, f_bias[n][None])
            z_first = 0
        return z.reshape(B, L, D)
    kk = jnp.arange(N, dtype=jnp.int32)
    c, s = _cos_sin(kk[:, None] * kk[None, :], N)
    g_real = jnp.concatenate([c, -s], 0)[None]
    ch, sh = c[:, :L], s[:, :L]
    g_fwd = jnp.concatenate([jnp.concatenate([ch, sh], 1), jnp.concatenate([-sh, ch], 1)], 0)[None]
    ci, si = ch.T, sh.T
    g_inv = (jnp.concatenate([jnp.concatenate([ci, -si], 1), jnp.concatenate([si, ci], 1)], 0) / N)[None]
    z = vx[0]
    for n in range(HY_ORDER):
        kf = group_spectrum(k2[n].reshape(1, 1, N, D), g_real)
        y = group_conv(z.reshape(B // 2, 2, L, D), kf, g_fwd, g_inv).reshape(B, L, D)
        z = vx[n + 1] * (y + z * f_bias[n])
    return z


def _adaln(cond, w_mod, b_mod):
    m = jax.nn.silu(cond) @ w_mod + b_mod
    return jnp.split(m, 6, axis=-1)


def _hyena_filter_taps(L, f_w1, f_b1, f_w2, f_b2, f_w3, f_freq):
    row = jnp.arange(2 * L, dtype=jnp.int32)
    pos = jnp.where(row < L, row, 2 * L - row)
    t = jnp.linspace(0.0, 1.0, L, dtype=F32)[jnp.minimum(pos, L - 1)][:, None]
    w = (2.0 * math.pi / L) * pos.astype(F32)[:, None]
    f = jnp.linspace(1e-4, HY_BANDS - 1, HY_BANDS, dtype=F32)[None, :]
    z = jnp.concatenate([t, jnp.cos(f * w), -jnp.sin(f * w)], axis=-1)
    h = jnp.sin(f_freq[0] * (z @ f_w1 + f_b1))
    h = jnp.sin(f_freq[1] * (h @ f_w2 + f_b2))
    h = (h @ f_w3).reshape(2 * L, HY_ORDER, 2, D_MODEL)
    deltas = jnp.abs(jnp.linspace(HY_MIN_DECAY, HY_MAX_DECAY, D_MODEL, dtype=F32))
    window = jnp.where(row[:, None] == L, 0.0, jnp.exp(-t * deltas))
    taps = []
    for n in range(HY_ORDER):
        k = jnp.where(row[:, None] < L, h[:, n, 0], h[:, n, 1]) * window
        taps.append(k / (jnp.sum(jnp.abs(k), axis=0, keepdims=True) + RMS_EPS))
    return jnp.stack(taps)


def _hyena_core(p, conv_w, conv_b, f_w1, f_b1, f_w2, f_b2, f_w3, f_freq, f_bias):
    L = p.shape[1]
    vx = short_conv(p, conv_w, conv_b)
    k2 = _hyena_filter_taps(L, f_w1, f_b1, f_w2, f_b2, f_w3, f_freq)
    return _hyena_long_convs(vx, k2, f_bias)


def _axial_rope_tables(L):
    rows = L // GRID_W
    row = jnp.broadcast_to(jnp.arange(rows, dtype=F32)[:, None], (rows, GRID_W)).reshape(L)
    col = jnp.broadcast_to(jnp.arange(GRID_W, dtype=F32)[None, :], (rows, GRID_W)).reshape(L)
    n_freq = HEAD_DIM // 4
    inv_freq = ROPE_THETA ** (-jnp.arange(n_freq, dtype=F32) / n_freq)
    ang = jnp.concatenate([row[:, None] * inv_freq, col[:, None] * inv_freq], axis=-1)
    return jnp.cos(ang), jnp.sin(ang)


def _expert_choice_ffn(x, shift, scale, w_router_t, w_gate, w_up, w_down, layer):
    B, n, D = x.shape
    cap = CAPACITY_FACTOR * n // N_EXPERTS
    aff = router(x, shift, scale, w_router_t)
    gate, idx = lax.top_k(aff, cap)
    xg = gather_rows(x, shift, scale, idx).reshape(N_EXPERTS, B * cap, D)
    gt = jnp.swapaxes(gate, 0, 1).reshape(N_EXPERTS, B * cap, 1)
    y = expert_ffn(xg, w_gate, w_up, w_down, layer, gt)
    return scatter_add_rows(y, idx, n)


def kernel(x, c, ctx, c_ctx, mod_w, mod_b, ln_g, ln_b, hy_w_in, hy_conv_w, hy_conv_b, hy_f_w1, hy_f_b1, hy_f_w2, hy_f_b2, hy_f_w3, hy_f_freq, hy_f_bias, hy_w_out, at_w_qkv, at_q_gain, at_k_gain, at_w_out, cm_w_in, cm_ln_g, cm_ln_b, cm_w_s, cm_b_s, cm_w_out, moe_router, moe_w_gate, moe_w_up, moe_w_down):
    B, L, D = x.shape
    Lc = ctx.shape[1]
    x_lat, x_ctx = x, ctx
    for i in range(DEPTH):
        kind, slot = LAYER_KINDS[i], KIND_SLOT[i]
        ctx_read = i <= LAST_CTX_LAYER
        ctx_update = i < LAST_CTX_LAYER
        sh1, sc1, g1, sh2, sc2, g2 = _adaln(c[:, None, :], mod_w[i], mod_b[i])
        if ctx_read:
            cmod = [jnp.broadcast_to(m[None, None, :], (B, 1, D)) for m in _adaln(c_ctx, mod_w[i], mod_b[i])]
        lg0, lb0, lg1, lb1 = ln_g[i, 0][None], ln_b[i, 0][None], ln_g[i, 1][None], ln_b[i, 1][None]

        if kind == 0:
            w_in = hy_w_in[slot].astype(BF16)
            w_out = hy_w_out[slot].astype(BF16)
            hy = (hy_conv_w[slot], hy_conv_b[slot], hy_f_w1[slot], hy_f_b1[slot], hy_f_w2[slot], hy_f_b2[slot],
                  hy_f_w3[slot], hy_f_freq[slot], hy_f_bias[slot])
            z = _hyena_core(mod_matmul(x_lat, sh1, sc1, w_in), *hy)
            x_lat = matmul_postnorm(z, w_out, x_lat, g1, lg0, lb0)
            if ctx_update:
                zc = _hyena_core(mod_matmul(x_ctx, cmod[0], cmod[1], w_in), *hy)
                x_ctx = matmul_postnorm(zc, w_out, x_ctx, cmod[2], lg0, lb0)
        elif kind == 1:
            w_qkv = at_w_qkv[slot].astype(BF16)
            w_out = at_w_out[slot].astype(BF16).reshape(N_Q_HEADS, HEAD_DIM, D)
            cos, sin = _axial_rope_tables(L)
            cos_t = jnp.tile(cos, (1, 128 // cos.shape[1]))
            sin_t = jnp.tile(jnp.concatenate([-sin, sin], axis=-1), (1, 128 // HEAD_DIM))
            qkv = mod_matmul(x_lat, sh1, sc1, w_qkv)
            q, k, v = qkv_prep(qkv, at_q_gain[slot], at_k_gain[slot], N_Q_HEADS, N_KV_HEADS, (cos_t, sin_t))
            kvc = mod_matmul(x_ctx, cmod[0], cmod[1], w_qkv[:, Q_COLS:])
            ck, cv = qkv_prep(kvc, None, at_k_gain[slot], 0, N_KV_HEADS, None)
            o = attention(q, ck, cv, k, v)
            x_lat = heads_matmul_postnorm(o, w_out, x_lat, g1, lg0, lb0)
        else:
            w_in = cm_w_in[slot].astype(BF16)
            w_out = cm_w_out[slot].astype(BF16)
            x_lat = gmlp_block(x_lat, sh1, sc1, w_in, cm_ln_g[slot][None], cm_ln_b[slot][None],
                               cm_w_s[slot].astype(BF16), cm_b_s[slot].T, w_out, g1, lg0, lb0)

        moe = (moe_router[i].T.astype(BF16), moe_w_gate, moe_w_up, moe_w_down, i)
        x_lat = postnorm(_expert_choice_ffn(x_lat, sh2, sc2, *moe), x_lat, g2, lg1, lb1)
        if ctx_update:
            x_ctx = postnorm(_expert_choice_ffn(x_ctx, cmod[3], cmod[4], *moe), x_ctx, cmod[5], lg1, lb1)
    return x_lat
```

```python
import math
from functools import partial

import jax
import jax.numpy as jnp
from jax import lax
from jax.experimental import pallas as pl
from jax.experimental.pallas import tpu as pltpu

F32 = jnp.float32
BF16 = jnp.bfloat16

D_MODEL = 1024
DEPTH = 4
GRID_W = 64
N_MIXERS = 3
LAYER_KINDS = tuple(i % N_MIXERS for i in range(DEPTH))
KIND_SLOT = tuple(LAYER_KINDS[:i].count(LAYER_KINDS[i]) for i in range(DEPTH))
LAST_CTX_LAYER = max([i for i, k in enumerate(LAYER_KINDS) if k == 1], default=-1)

HY_ORDER = 2
HY_BANDS = 16
HY_DECAY_TARGET = 1e-2
HY_FAST_DECAY_PCT = 0.3
HY_SLOW_DECAY_PCT = 1.5
HY_MIN_DECAY = math.log(HY_DECAY_TARGET) / HY_SLOW_DECAY_PCT
HY_MAX_DECAY = math.log(HY_DECAY_TARGET) / HY_FAST_DECAY_PCT

HEAD_DIM = 64
V_EXT = 128
N_Q_HEADS = D_MODEL // HEAD_DIM
N_KV_HEADS = 4
GQA_GROUP = N_Q_HEADS // N_KV_HEADS
ROPE_THETA = 10000.0
ATTN_SCALE = HEAD_DIM ** -0.5
Q_COLS = N_Q_HEADS * HEAD_DIM

CHUNK = 128
GMLP_HALF = 2 * D_MODEL
GMLP_GROUPS = 16
GMLP_GROUP_W = GMLP_HALF // GMLP_GROUPS

N_EXPERTS = 16
CAPACITY_FACTOR = 2

RMS_EPS = 1e-6
LN_EPS = 1e-5
DEEPNORM_ALPHA = (2 * DEPTH) ** 0.25

VMEM_LIMIT_BYTES = 56 * 1024 * 1024
OUT_TILE_BYTES = 4 * 1024 * 1024


def _row_tile(rows, n_cols):
    tm = 512
    while tm > 8 and (tm * n_cols * 4 > OUT_TILE_BYTES or rows % tm):
        tm //= 2
    return tm


def _params(sem):
    return pltpu.CompilerParams(dimension_semantics=sem, vmem_limit_bytes=VMEM_LIMIT_BYTES)


def _gelu_erf(x):
    return 0.5 * x * (1.0 + lax.erf(x * (2.0 ** -0.5)))


def _mod_matmul_kernel(x_ref, sh_ref, sc_ref, w_ref, o_ref):
    h = x_ref[0] * (1.0 + sc_ref[0]) + sh_ref[0]
    o_ref[0] = jnp.dot(h.astype(BF16), w_ref[...], preferred_element_type=F32)


def mod_matmul(x, shift, scale, w):
    B, L, K = x.shape
    N = w.shape[1]
    tm = _row_tile(L, N)
    return pl.pallas_call(
        _mod_matmul_kernel,
        grid=(B, L // tm),
        in_specs=[
            pl.BlockSpec((1, tm, K), lambda b, i: (b, i, 0)),
            pl.BlockSpec((1, 1, K), lambda b, i: (b, 0, 0)),
            pl.BlockSpec((1, 1, K), lambda b, i: (b, 0, 0)),
            pl.BlockSpec((K, N), lambda b, i: (0, 0)),
        ],
        out_specs=pl.BlockSpec((1, tm, N), lambda b, i: (b, i, 0)),
        out_shape=jax.ShapeDtypeStruct((B, L, N), F32),
        compiler_params=_params(("parallel", "parallel")),
        name="mod_matmul",
    )(x, shift, scale, w)


def _layer_norm_rows(t, g, b):
    mu = jnp.mean(t, axis=-1, keepdims=True)
    d = t - mu
    var = jnp.mean(d * d, axis=-1, keepdims=True)
    return d * lax.rsqrt(var + LN_EPS) * g + b


def _matmul_postnorm_kernel(z_ref, w_ref, x_ref, gate_ref, g_ref, b_ref, o_ref):
    y = jnp.dot(z_ref[0].astype(BF16), w_ref[...], preferred_element_type=F32)
    t = DEEPNORM_ALPHA * x_ref[0] + gate_ref[0] * y
    o_ref[0] = _layer_norm_rows(t, g_ref[...], b_ref[...])


def matmul_postnorm(z, w, x, gate, g, b):
    B, L, K = z.shape
    D = w.shape[1]
    tm = _row_tile(L, max(K, D))
    return pl.pallas_call(
        _matmul_postnorm_kernel,
        grid=(B, L // tm),
        in_specs=[
            pl.BlockSpec((1, tm, K), lambda bi, i: (bi, i, 0)),
            pl.BlockSpec((K, D), lambda bi, i: (0, 0)),
            pl.BlockSpec((1, tm, D), lambda bi, i: (bi, i, 0)),
            pl.BlockSpec((1, 1, D), lambda bi, i: (bi, 0, 0)),
            pl.BlockSpec((1, D), lambda bi, i: (0, 0)),
            pl.BlockSpec((1, D), lambda bi, i: (0, 0)),
        ],
        out_specs=pl.BlockSpec((1, tm, D), lambda bi, i: (bi, i, 0)),
        out_shape=jax.ShapeDtypeStruct((B, L, D), F32),
        compiler_params=_params(("parallel", "parallel")),
        name="matmul_postnorm",
    )(z, w, x, gate, g, b)


def _postnorm_kernel(y_ref, x_ref, gate_ref, g_ref, b_ref, o_ref):
    t = DEEPNORM_ALPHA * x_ref[0] + gate_ref[0] * y_ref[0]
    o_ref[0] = _layer_norm_rows(t, g_ref[...], b_ref[...])


def postnorm(y, x, gate, g, b):
    B, L, D = x.shape
    tm = _row_tile(L, D)
    return pl.pallas_call(
        _postnorm_kernel,
        grid=(B, L // tm),
        in_specs=[
            pl.BlockSpec((1, tm, D), lambda bi, i: (bi, i, 0)),
            pl.BlockSpec((1, tm, D), lambda bi, i: (bi, i, 0)),
            pl.BlockSpec((1, 1, D), lambda bi, i: (bi, 0, 0)),
            pl.BlockSpec((1, D), lambda bi, i: (0, 0)),
            pl.BlockSpec((1, D), lambda bi, i: (0, 0)),
        ],
        out_specs=pl.BlockSpec((1, tm, D), lambda bi, i: (bi, i, 0)),
        out_shape=jax.ShapeDtypeStruct((B, L, D), F32),
        compiler_params=_params(("parallel", "parallel")),
        name="postnorm",
    )(y, x, gate, g, b)


def _gmlp_block_kernel(x_ref, sh_ref, sc_ref, win_ref, lng_ref, lnb_ref, ws_ref, bs_ref, w_ref, gate_ref, g_ref, b_ref,
                       o_ref):
    h = (x_ref[0] * (1.0 + sc_ref[0]) + sh_ref[0]).astype(BF16)
    z = _gelu_erf(jnp.dot(h, win_ref[...], preferred_element_type=F32))
    tm, half = z.shape[0], z.shape[1] // 2
    u, v = z[:, :half], z[:, half:]
    vn = _layer_norm_rows(v, lng_ref[...], lnb_ref[...]).astype(BF16)
    n_grp = ws_ref.shape[0]
    gw = half // n_grp
    chunks = []
    for c in range(tm // CHUNK):
        cols = []
        for g in range(n_grp):
            blk = vn[c * CHUNK:(c + 1) * CHUNK, g * gw:(g + 1) * gw]
            cols.append(jnp.dot(ws_ref[g], blk, preferred_element_type=F32) + bs_ref[:, g:g + 1])
        chunks.append(jnp.concatenate(cols, axis=1))
    sv = jnp.concatenate(chunks, axis=0)
    y = jnp.dot((u * sv).astype(BF16), w_ref[...], preferred_element_type=F32)
    t = DEEPNORM_ALPHA * x_ref[0] + gate_ref[0] * y
    o_ref[0] = _layer_norm_rows(t, g_ref[...], b_ref[...])


def gmlp_block(x, shift, scale, w_in, ln_g, ln_b, w_s, b_s_t, w_out, gate, g, b, tm=256):
    B, L, D = x.shape
    H = w_out.shape[0]
    const = lambda shape: pl.BlockSpec(shape, lambda bi, i: (0,) * len(shape))
    vec = pl.BlockSpec((1, 1, D), lambda bi, i: (bi, 0, 0))
    return pl.pallas_call(
        _gmlp_block_kernel,
        grid=(B, L // tm),
        in_specs=[
            pl.BlockSpec((1, tm, D), lambda bi, i: (bi, i, 0)), vec, vec, const((D, 2 * H)),
            const((1, H)), const((1, H)), const(w_s.shape), const(b_s_t.shape), const((H, D)),
            vec, const((1, D)), const((1, D)),
        ],
        out_specs=pl.BlockSpec((1, tm, D), lambda bi, i: (bi, i, 0)),
        out_shape=jax.ShapeDtypeStruct((B, L, D), F32),
        compiler_params=_params(("parallel", "parallel")),
        name="gmlp_block",
    )(x, shift, scale, w_in, ln_g, ln_b, w_s, b_s_t, w_out, gate, g, b)


def _head_rms_norm(x, ones_bd, gain):
    sq = x * x
    sq_hi = sq.astype(BF16)
    sq_lo = (sq - sq_hi.astype(F32)).astype(BF16)
    ss = jnp.dot(sq_hi, ones_bd, preferred_element_type=F32) + jnp.dot(sq_lo, ones_bd, preferred_element_type=F32)
    return x * lax.rsqrt(ss * (1.0 / HEAD_DIM) + RMS_EPS) * gain


def _rope(x, cos_t, sin_t):
    w = x.shape[1]
    lane = lax.broadcasted_iota(jnp.int32, x.shape, 1)
    half = HEAD_DIM // 2
    partner = jnp.where((lane & (HEAD_DIM - 1)) < half, pltpu.roll(x, w - half, 1), pltpu.roll(x, half, 1))
    reps = w // cos_t.shape[1]
    return x * jnp.tile(cos_t, (1, reps)) + partner * jnp.tile(sin_t, (1, reps))


def _qkv_prep_kernel(*refs, n_q, rope):
    refs = list(refs)
    x_ref = refs.pop(0)
    cos_ref, sin_ref = (refs.pop(0), refs.pop(0)) if rope else (None, None)
    if n_q:
        qg_ref, qbd_ref = refs.pop(0), refs.pop(0)
    kg_ref, kbd_ref = refs.pop(0), refs.pop(0)
    if n_q:
        q_out = refs.pop(0)
    k_out, v_out = refs
    x = x_ref[0]
    n_kv = k_out.shape[1]
    qc, kc = n_q * HEAD_DIM, n_kv * HEAD_DIM
    if n_q:
        q = _head_rms_norm(x[:, :qc], qbd_ref[...], qg_ref[...])
        if rope:
            q = _rope(q, cos_ref[...], sin_ref[...])
        q = q * ATTN_SCALE
        for h in range(n_q):
            q_out[0, h] = q[:, h * HEAD_DIM:(h + 1) * HEAD_DIM].astype(BF16)
    k = _head_rms_norm(x[:, qc:qc + kc], kbd_ref[...], kg_ref[...])
    if rope:
        k = _rope(k, cos_ref[...], sin_ref[...])
    lane = lax.broadcasted_iota(jnp.int32, (x.shape[0], V_EXT), 1)
    for h in range(n_kv):
        k_out[0, h] = k[:, h * HEAD_DIM:(h + 1) * HEAD_DIM].astype(BF16)
        c0 = qc + kc + h * HEAD_DIM
        base = c0 // V_EXT * V_EXT
        vv = x[:, base:base + V_EXT]
        if c0 != base:
            vv = pltpu.roll(vv, V_EXT - (c0 - base), 1)
        v_out[0, h] = jnp.where(lane < HEAD_DIM, vv, (lane == HEAD_DIM).astype(F32)).astype(BF16)


def qkv_prep(x, q_gain, k_gain, n_q, n_kv, rope_tables, tm=256):
    B, L, C = x.shape
    hd = HEAD_DIM
    tm = min(tm, L)

    def ones_bd(n):
        r = jnp.arange(n * hd) // hd
        return (r[:, None] == r[None, :]).astype(BF16)

    const = lambda shape: pl.BlockSpec(shape, lambda b, i: (0,) * len(shape))
    args, specs = [x], [pl.BlockSpec((1, tm, C), lambda b, i: (b, i, 0))]
    if rope_tables is not None:
        args += list(rope_tables)
        specs += [pl.BlockSpec((tm, 128), lambda b, i: (i, 0))] * 2
    if n_q:
        args += [jnp.tile(q_gain, n_q)[None], ones_bd(n_q)]
        specs += [const((1, n_q * hd)), const((n_q * hd, n_q * hd))]
    args += [jnp.tile(k_gain, n_kv)[None], ones_bd(n_kv)]
    specs += [const((1, n_kv * hd)), const((n_kv * hd, n_kv * hd))]
    head_spec = lambda n, w=hd: pl.BlockSpec((1, n, tm, w), lambda b, i: (b, 0, i, 0))
    head_shape = lambda n, w=hd: jax.ShapeDtypeStruct((B, n, L, w), BF16)
    out_specs = ([head_spec(n_q)] if n_q else []) + [head_spec(n_kv), head_spec(n_kv, V_EXT)]
    out_shape = ([head_shape(n_q)] if n_q else []) + [head_shape(n_kv), head_shape(n_kv, V_EXT)]
    return pl.pallas_call(
        partial(_qkv_prep_kernel, n_q=n_q, rope=rope_tables is not None),
        grid=(B, L // tm),
        in_specs=specs,
        out_specs=out_specs,
        out_shape=out_shape,
        compiler_params=_params(("parallel", "parallel")),
        name="qkv_prep",
    )(*args)


def _nt_dot(a, b):
    return lax.dot_general(a, b, (((1,), (1,)), ((), ())), preferred_element_type=F32)


def _attention_kernel(q_ref, ck_ref, cv_ref, k_ref, v_ref, o_ref):
    g, hd = q_ref.shape[1], q_ref.shape[3]
    for h in range(g):
        q = q_ref[0, h]
        s_c = _nt_dot(q, ck_ref[0, 0])
        s_l = _nt_dot(q, k_ref[0, 0])
        m = jnp.maximum(jnp.max(s_c, axis=-1, keepdims=True), jnp.max(s_l, axis=-1, keepdims=True))
        p_c = jnp.exp(s_c - m).astype(BF16)
        p_l = jnp.exp(s_l - m).astype(BF16)
        o = (jnp.dot(p_c, cv_ref[0, 0], preferred_element_type=F32)
             + jnp.dot(p_l, v_ref[0, 0], preferred_element_type=F32))
        o_ref[0, h] = (o[:, :hd] / o[:, hd:hd + 1]).astype(BF16)


def attention(q, ck, cv, k, v, tq=256):
    B, H, L, hd = q.shape
    KV, Lc = ck.shape[1], ck.shape[2]
    G = H // KV
    kv_spec = lambda n, w=hd: pl.BlockSpec((1, 1, n, w), lambda b, kv, i: (b, kv, 0, 0))
    q_spec = pl.BlockSpec((1, G, tq, hd), lambda b, kv, i: (b, kv, i, 0))
    return pl.pallas_call(
        _attention_kernel,
        grid=(B, KV, L // tq),
        in_specs=[q_spec, kv_spec(Lc), kv_spec(Lc, V_EXT), kv_spec(L), kv_spec(L, V_EXT)],
        out_specs=q_spec,
        out_shape=jax.ShapeDtypeStruct((B, H, L, hd), BF16),
        compiler_params=_params(("parallel", "parallel", "parallel")),
        name="gqa_attention",
    )(q, ck, cv, k, v)


def _heads_matmul_postnorm_kernel(o_ref, w_ref, x_ref, gate_ref, g_ref, b_ref, out_ref):
    y = jnp.dot(o_ref[0, 0], w_ref[0], preferred_element_type=F32)
    for h in range(1, o_ref.shape[1]):
        y += jnp.dot(o_ref[0, h], w_ref[h], preferred_element_type=F32)
    t = DEEPNORM_ALPHA * x_ref[0] + gate_ref[0] * y
    out_ref[0] = _layer_norm_rows(t, g_ref[...], b_ref[...])


def heads_matmul_postnorm(o, w, x, gate, g, b, tm=512):
    B, H, L, hd = o.shape
    D = w.shape[2]
    return pl.pallas_call(
        _heads_matmul_postnorm_kernel,
        grid=(B, L // tm),
        in_specs=[
            pl.BlockSpec((1, H, tm, hd), lambda bi, i: (bi, 0, i, 0)),
            pl.BlockSpec((H, hd, D), lambda bi, i: (0, 0, 0)),
            pl.BlockSpec((1, tm, D), lambda bi, i: (bi, i, 0)),
            pl.BlockSpec((1, 1, D), lambda bi, i: (bi, 0, 0)),
            pl.BlockSpec((1, D), lambda bi, i: (0, 0)),
            pl.BlockSpec((1, D), lambda bi, i: (0, 0)),
        ],
        out_specs=pl.BlockSpec((1, tm, D), lambda bi, i: (bi, i, 0)),
        out_shape=jax.ShapeDtypeStruct((B, L, D), F32),
        compiler_params=_params(("parallel", "parallel")),
        name="heads_matmul_postnorm",
    )(o, w, x, gate, g, b)


def _router_kernel(x_ref, sh_ref, sc_ref, w_ref, o_ref):
    h = (x_ref[0] * (1.0 + sc_ref[0]) + sh_ref[0]).astype(BF16)
    logits = _nt_dot(w_ref[...], h)
    e = jnp.exp(logits - jnp.max(logits, axis=0, keepdims=True))
    o_ref[0] = e / jnp.sum(e, axis=0, keepdims=True)


def router(x, shift, scale, w_t):
    B, n, D = x.shape
    E = w_t.shape[0]
    tm = min(n, 512)
    vec = pl.BlockSpec((1, 1, D), lambda b, i: (b, 0, 0))
    return pl.pallas_call(
        _router_kernel,
        grid=(B, n // tm),
        in_specs=[pl.BlockSpec((1, tm, D), lambda b, i: (b, i, 0)), vec, vec, pl.BlockSpec((E, D), lambda b, i: (0, 0))],
        out_specs=pl.BlockSpec((1, E, tm), lambda b, i: (b, 0, i)),
        out_shape=jax.ShapeDtypeStruct((B, E, n), F32),
        compiler_params=_params(("parallel", "parallel")),
        name="moe_router",
    )(x, shift, scale, w_t)


ROUTE_UNROLL = 8


def _gather_rows_kernel(idx_ref, x_ref, sh_ref, sc_ref, o_ref, buf):
    cap = buf.shape[0]
    base = (pl.program_id(0) * pl.num_programs(1) + pl.program_id(1)) * cap

    def body(s, carry):
        for u in range(ROUTE_UNROLL):
            r = s * ROUTE_UNROLL + u
            buf[pl.ds(r, 1), :] = x_ref[pl.ds(idx_ref[base + r], 1), :]
        return carry

    lax.fori_loop(0, cap // ROUTE_UNROLL, body, 0)
    o_ref[...] = (buf[...] * (1.0 + sc_ref[...]) + sh_ref[...]).astype(BF16)


def gather_rows(x, shift, scale, idx):
    B, n, D = x.shape
    _, E, cap = idx.shape
    vec = pl.BlockSpec((None, 1, D), lambda b, e, idx: (b, 0, 0))
    return pl.pallas_call(
        _gather_rows_kernel,
        grid_spec=pltpu.PrefetchScalarGridSpec(
            num_scalar_prefetch=1,
            grid=(B, E),
            in_specs=[pl.BlockSpec((None, n, D), lambda b, e, idx: (b, 0, 0)), vec, vec],
            out_specs=pl.BlockSpec((None, None, cap, D), lambda b, e, idx: (e, b, 0, 0)),
            scratch_shapes=[pltpu.VMEM((cap, D), F32)],
        ),
        out_shape=jax.ShapeDtypeStruct((E, B, cap, D), BF16),
        compiler_params=_params(("parallel", "arbitrary")),
        name="moe_gather_rows",
    )(idx.reshape(-1), x, shift, scale)


def _scatter_add_kernel(idx_ref, y_ref, o_ref):
    cap = y_ref.shape[0]
    base = (pl.program_id(0) * pl.num_programs(1) + pl.program_id(1)) * cap

    @pl.when(pl.program_id(1) == 0)
    def _():
        o_ref[...] = jnp.zeros(o_ref.shape, o_ref.dtype)

    def body(s, carry):
        rows = [idx_ref[base + s * ROUTE_UNROLL + u] for u in range(ROUTE_UNROLL)]
        sums = [o_ref[pl.ds(rows[u], 1), :] + y_ref[pl.ds(s * ROUTE_UNROLL + u, 1), :] for u in range(ROUTE_UNROLL)]
        for u in range(ROUTE_UNROLL):
            o_ref[pl.ds(rows[u], 1), :] = sums[u]
        return carry

    lax.fori_loop(0, cap // ROUTE_UNROLL, body, 0)


def scatter_add_rows(y, idx, n):
    E, R, D = y.shape
    B, _, cap = idx.shape
    return pl.pallas_call(
        _scatter_add_kernel,
        grid_spec=pltpu.PrefetchScalarGridSpec(
            num_scalar_prefetch=1,
            grid=(B, E),
            in_specs=[pl.BlockSpec((None, cap, D), lambda b, e, idx: (e, b, 0))],
            out_specs=pl.BlockSpec((None, n, D), lambda b, e, idx: (b, 0, 0)),
        ),
        out_shape=jax.ShapeDtypeStruct((B, n, D), F32),
        compiler_params=_params(("parallel", "arbitrary")),
        name="moe_scatter_add_rows",
    )(idx.reshape(-1), y)


def _expert_ffn_kernel(x_ref, wg_ref, wu_ref, wd_ref, gate_ref, o_ref):
    f = pl.program_id(1)
    x = x_ref[0]
    a = jnp.dot(x, wg_ref[0].astype(BF16), preferred_element_type=F32)
    u = jnp.dot(x, wu_ref[0].astype(BF16), preferred_element_type=F32)
    h = (a * jax.nn.sigmoid(a) * u).astype(BF16)
    y = jnp.dot(h, wd_ref[0].astype(BF16), preferred_element_type=F32)

    @pl.when(f == 0)
    def _():
        o_ref[0] = y

    @pl.when(f > 0)
    def _():
        o_ref[0] += y

    @pl.when(f == pl.num_programs(1) - 1)
    def _():
        o_ref[0] *= gate_ref[0]


def expert_ffn(xg, w_gate, w_up, w_down, layer, gate, tf=512):
    E, R, D = xg.shape
    F = w_gate.shape[3]
    return pl.pallas_call(
        _expert_ffn_kernel,
        grid=(E, F // tf),
        in_specs=[
            pl.BlockSpec((1, R, D), lambda e, f: (e, 0, 0)),
            pl.BlockSpec((None, 1, D, tf), lambda e, f: (layer, e, 0, f)),
            pl.BlockSpec((None, 1, D, tf), lambda e, f: (layer, e, 0, f)),
            pl.BlockSpec((None, 1, tf, D), lambda e, f: (layer, e, f, 0)),
            pl.BlockSpec((1, R, 1), lambda e, f: (e, 0, 0)),
        ],
        out_specs=pl.BlockSpec((1, R, D), lambda e, f: (e, 0, 0)),
        out_shape=jax.ShapeDtypeStruct((E, R, D), F32),
        compiler_params=_params(("parallel", "arbitrary")),
        name="expert_ffn",
    )(xg, w_gate, w_up, w_down, gate)


def _short_conv_kernel(p_ref, w_ref, b_ref, o_ref):
    q = p_ref[0]
    L = q.shape[0]
    row = lax.broadcasted_iota(jnp.int32, q.shape, 0)
    prev = jnp.where(row == 0, 0.0, pltpu.roll(q, 1, 0))
    nxt = jnp.where(row == L - 1, 0.0, pltpu.roll(q, L - 1, 0))
    w = w_ref[...]
    o_ref[0, 0] = prev * w[0:1] + q * w[1:2] + nxt * w[2:3] + b_ref[...]


def short_conv(p, conv_w, conv_b, tc=256):
    B, L, D3 = p.shape
    D = D3 // 3
    nj = D // tc
    return pl.pallas_call(
        _short_conv_kernel,
        grid=(3, B, nj),
        in_specs=[
            pl.BlockSpec((1, L, tc), lambda s, b, j: (b, 0, s * nj + j)),
            pl.BlockSpec((3, tc), lambda s, b, j: (0, s * nj + j)),
            pl.BlockSpec((1, tc), lambda s, b, j: (0, s * nj + j)),
        ],
        out_specs=pl.BlockSpec((1, 1, L, tc), lambda s, b, j: (s, b, 0, j)),
        out_shape=jax.ShapeDtypeStruct((3, B, L, D), F32),
        compiler_params=_params(("parallel", "parallel", "parallel")),
        name="hyena_short_conv",
    )(p, conv_w, conv_b[None])


FFT_GROUP = 128
SUBLANES = 8
FFT_DATA_PASSES = 1
FFT_FILTER_PASSES = 3


def _split_hi_lo(m):
    hi = m.astype(BF16)
    return hi, (m - hi.astype(F32)).astype(BF16)


def _dot_split(m_hi, m_lo, x, passes):
    x_hi = x.astype(BF16)
    acc = jnp.dot(m_hi, x_hi, preferred_element_type=F32)
    if passes >= 2:
        x_lo = (x - x_hi.astype(F32)).astype(BF16)
        acc += jnp.dot(m_hi, x_lo, preferred_element_type=F32)
    if passes >= 3:
        acc += jnp.dot(m_lo, x_hi, preferred_element_type=F32)
    return acc


def _cos_sin(num, den):
    ang = (2.0 * math.pi / den) * (num % den).astype(F32)
    return jnp.cos(ang), jnp.sin(ang)


def _outer_tables(N1, n_in):
    k1 = jnp.arange(N1, dtype=jnp.int32)
    c, s = _cos_sin(k1[:, None] * jnp.arange(n_in, dtype=jnp.int32)[None, :], N1)
    fwd = jnp.concatenate([jnp.concatenate([c, s], 1), jnp.concatenate([-s, c], 1)], 0)
    cf, sf = _cos_sin(k1[:, None] * k1[None, :], N1)
    fwd_real = jnp.concatenate([cf, -sf], 0)
    ci, si = c.T, s.T
    inv = jnp.concatenate([jnp.concatenate([ci, -si], 1), jnp.concatenate([si, ci], 1)], 0) / (N1 * FFT_GROUP)
    return fwd, fwd_real, inv


def _group_tables(N1):
    G = FFT_GROUP
    k = jnp.arange(N1, dtype=jnp.int32)[:, None, None] + N1 * jnp.arange(G, dtype=jnp.int32)[None, :, None]
    gr, gs = _cos_sin(k * jnp.arange(G, dtype=jnp.int32)[None, None, :], N1 * G)
    gi = -gs
    fwd = jnp.concatenate([jnp.concatenate([gr, -gi], 2), jnp.concatenate([gi, gr], 2)], 1)
    return fwd, jnp.swapaxes(fwd, 1, 2)


def _tile_expand(m):
    r, c = m.shape
    eye = jnp.eye(SUBLANES, dtype=m.dtype)
    return (m[:, None, :, None] * eye[None, :, None, :]).reshape(r * SUBLANES, c * SUBLANES)


def _outer_fwd_kernel(z_ref, m_hi, m_lo, o_ref, *, passes):
    parts, n_in, sub, d = z_ref.shape
    x = z_ref[...].reshape(parts * n_in * sub, d)
    r = _dot_split(m_hi[...], m_lo[...], x, passes)
    o_ref[0] = r.reshape(2, r.shape[0] // (2 * sub), sub, d)


def outer_fwd(z, m, parts, first, n_seq):
    _, n_in, G, D = z.shape
    N1 = m.shape[0] // 2
    m_hi, m_lo = _split_hi_lo(_tile_expand(m))
    first_blk = first // parts
    return pl.pallas_call(
        partial(_outer_fwd_kernel, passes=FFT_DATA_PASSES if parts == 2 else FFT_FILTER_PASSES),
        grid=(n_seq, G // SUBLANES),
        in_specs=[
            pl.BlockSpec((parts, n_in, SUBLANES, D), lambda p, j: (first_blk + p, 0, j, 0)),
            pl.BlockSpec(m_hi.shape, lambda p, j: (0, 0)),
            pl.BlockSpec(m_hi.shape, lambda p, j: (0, 0)),
        ],
        out_specs=pl.BlockSpec((1, 2, N1, SUBLANES, D), lambda p, j: (p, 0, 0, j, 0)),
        out_shape=jax.ShapeDtypeStruct((n_seq, 2, N1, G, D), F32),
        compiler_params=_params(("parallel", "parallel")),
        name="hyena_dft_outer_fwd",
    )(z, m_hi, m_lo)


def _group_spectrum_kernel(a_ref, g_hi, g_lo, o_ref):
    a = a_ref[0]
    x = _dot_split(g_hi[0], g_lo[0], a.reshape(a.shape[0] * a.shape[1], a.shape[2]), FFT_FILTER_PASSES)
    o_ref[...] = x.reshape(2, x.shape[0] // 2, x.shape[1])


def group_spectrum(a, g):
    _, parts, rows, D = a.shape
    n_grp, Rm2 = g.shape[0], g.shape[1]
    Rin = rows // n_grp
    g_hi, g_lo = _split_hi_lo(g)
    return pl.pallas_call(
        _group_spectrum_kernel,
        grid=(n_grp,),
        in_specs=[
            pl.BlockSpec((1, parts, Rin, D), lambda k: (0, 0, k, 0)),
            pl.BlockSpec((1,) + g.shape[1:], lambda k: (k, 0, 0)),
            pl.BlockSpec((1,) + g.shape[1:], lambda k: (k, 0, 0)),
        ],
        out_specs=pl.BlockSpec((2, Rm2 // 2, D), lambda k: (0, k, 0)),
        out_shape=jax.ShapeDtypeStruct((2, n_grp * Rm2 // 2, D), F32),
        compiler_params=_params(("parallel",)),
        name="hyena_filter_spectrum",
    )(a, g_hi, g_lo)


def _group_conv_kernel(a_ref, kf_ref, g_hi, g_lo, h_hi, h_lo, o_ref):
    n_grp = g_hi.shape[0]
    r_in, r_mid, r_out = a_ref.shape[2] // n_grp, kf_ref.shape[1] // n_grp, o_ref.shape[2] // n_grp
    for j in range(n_grp):
        a = jnp.concatenate([a_ref[0, 0, j * r_in:(j + 1) * r_in], a_ref[0, 1, j * r_in:(j + 1) * r_in]], axis=0)
        x = _dot_split(g_hi[j], g_lo[j], a, FFT_DATA_PASSES)
        xr, xi = x[:r_mid], x[r_mid:]
        kr, ki = kf_ref[0, j * r_mid:(j + 1) * r_mid], kf_ref[1, j * r_mid:(j + 1) * r_mid]
        y = jnp.concatenate([xr * kr - xi * ki, xr * ki + xi * kr], axis=0)
        p = _dot_split(h_hi[j], h_lo[j], y, FFT_DATA_PASSES)
        o_ref[0, 0, j * r_out:(j + 1) * r_out] = p[:r_out]
        o_ref[0, 1, j * r_out:(j + 1) * r_out] = p[r_out:]


def group_conv(a, kf, g, h):
    P, _, rows, D = a.shape
    n_grp, Rm2, Rout2 = g.shape[0], g.shape[1], h.shape[1]
    Rin = rows // n_grp
    gs = 2 if n_grp % 2 == 0 else 1
    g_hi, g_lo = _split_hi_lo(g)
    h_hi, h_lo = _split_hi_lo(h)
    gspec = pl.BlockSpec((gs,) + g.shape[1:], lambda k, p: (k, 0, 0))
    hspec = pl.BlockSpec((gs,) + h.shape[1:], lambda k, p: (k, 0, 0))
    return pl.pallas_call(
        _group_conv_kernel,
        grid=(n_grp // gs, P),
        in_specs=[
            pl.BlockSpec((1, 2, gs * Rin, D), lambda k, p: (p, 0, k, 0)),
            pl.BlockSpec((2, gs * Rm2 // 2, D), lambda k, p: (0, k, 0)),
            gspec, gspec, hspec, hspec,
        ],
        out_specs=pl.BlockSpec((1, 2, gs * Rout2 // 2, D), lambda k, p: (p, 0, k, 0)),
        out_shape=jax.ShapeDtypeStruct((P, 2, n_grp * Rout2 // 2, D), F32),
        compiler_params=_params(("parallel", "parallel")),
        name="hyena_group_conv",
    )(a, kf, g_hi, g_lo, h_hi, h_lo)


def _outer_inv_kernel(p_ref, m_hi, m_lo, gate_ref, u_ref, bias_ref, o_ref):
    _, _, n1, sub, d = p_ref.shape
    y = _dot_split(m_hi[...], m_lo[...], p_ref[0].reshape(2 * n1 * sub, d), FFT_DATA_PASSES)
    y = y.reshape(2, y.shape[0] // (2 * sub), sub, d)
    o_ref[...] = gate_ref[...] * (y + u_ref[...] * bias_ref[...])


def outer_inv(pm, m, gate, gate_first, u, u_first, bias):
    P, _, N1, G, D = pm.shape
    n_out = m.shape[0] // 2
    m_hi, m_lo = _split_hi_lo(_tile_expand(m))
    seq = lambda first: pl.BlockSpec((2, n_out, SUBLANES, D), lambda p, j: (first // 2 + p, 0, j, 0))
    return pl.pallas_call(
        _outer_inv_kernel,
        grid=(P, G // SUBLANES),
        in_specs=[
            pl.BlockSpec((1, 2, N1, SUBLANES, D), lambda p, j: (p, 0, 0, j, 0)),
            pl.BlockSpec(m_hi.shape, lambda p, j: (0, 0)),
            pl.BlockSpec(m_hi.shape, lambda p, j: (0, 0)),
            seq(gate_first), seq(u_first),
            pl.BlockSpec((1, D), lambda p, j: (0, 0)),
        ],
        out_specs=seq(0),
        out_shape=jax.ShapeDtypeStruct((2 * P, n_out, G, D), F32),
        compiler_params=_params(("parallel", "parallel")),
        name="hyena_dft_outer_inv",
    )(pm, m_hi, m_lo, gate, u, bias)


def _hyena_long_convs(vx, k2, f_bias):
    _, B, L, D = vx.shape
    N = 2 * L
    G = FFT_GROUP
    if N // G >= 16:
        N1, Nh = N // G, N // G // 2
        fwd, fwd_real, inv = _outer_tables(N1, Nh)
        g_fwd, g_inv = _group_tables(N1)
        vx4 = vx.reshape(3 * B, Nh, G, D)
        k4 = k2.reshape(HY_ORDER, N1, G, D)
        z, z_first = vx4, 0
        for n in range(HY_ORDER):
            kf = group_spectrum(outer_fwd(k4, fwd_real, 1, n, 1).reshape(1, 2, N, D), g_fwd)
            a = outer_fwd(z, fwd, 2, z_first, B // 2).reshape(B // 2, 2, N, D)
            pm = group_conv(a, kf, g_fwd, g_inv).reshape(B // 2, 2, N1, G, D)
            z = outer_inv(pm, inv, vx4, (n + 1) * B, z, z_first, f_bias[n][None])
            z_first = 0
        return z.reshape(B, L, D)
    kk = jnp.arange(N, dtype=jnp.int32)
    c, s = _cos_sin(kk[:, None] * kk[None, :], N)
    g_real = jnp.concatenate([c, -s], 0)[None]
    ch, sh = c[:, :L], s[:, :L]
    g_fwd = jnp.concatenate([jnp.concatenate([ch, sh], 1), jnp.concatenate([-sh, ch], 1)], 0)[None]
    ci, si = ch.T, sh.T
    g_inv = (jnp.concatenate([jnp.concatenate([ci, -si], 1), jnp.concatenate([si, ci], 1)], 0) / N)[None]
    z = vx[0]
    for n in range(HY_ORDER):
        kf = group_spectrum(k2[n].reshape(1, 1, N, D), g_real)
        y = group_conv(z.reshape(B // 2, 2, L, D), kf, g_fwd, g_inv).reshape(B, L, D)
        z = vx[n + 1] * (y + z * f_bias[n])
    return z


def _adaln(cond, w_mod, b_mod):
    m = jax.nn.silu(cond) @ w_mod + b_mod
    return jnp.split(m, 6, axis=-1)


def _hyena_filter_taps(L, f_w1, f_b1, f_w2, f_b2, f_w3, f_freq):
    row = jnp.arange(2 * L, dtype=jnp.int32)
    pos = jnp.where(row < L, row, 2 * L - row)
    t = jnp.linspace(0.0, 1.0, L, dtype=F32)[jnp.minimum(pos, L - 1)][:, None]
    w = (2.0 * math.pi / L) * pos.astype(F32)[:, None]
    f = jnp.linspace(1e-4, HY_BANDS - 1, HY_BANDS, dtype=F32)[None, :]
    z = jnp.concatenate([t, jnp.cos(f * w), -jnp.sin(f * w)], axis=-1)
    h = jnp.sin(f_freq[0] * (z @ f_w1 + f_b1))
    h = jnp.sin(f_freq[1] * (h @ f_w2 + f_b2))
    h = (h @ f_w3).reshape(2 * L, HY_ORDER, 2, D_MODEL)
    deltas = jnp.abs(jnp.linspace(HY_MIN_DECAY, HY_MAX_DECAY, D_MODEL, dtype=F32))
    window = jnp.where(row[:, None] == L, 0.0, jnp.exp(-t * deltas))
    taps = []
    for n in range(HY_ORDER):
        k = jnp.where(row[:, None] < L, h[:, n, 0], h[:, n, 1]) * window
        taps.append(k / (jnp.sum(jnp.abs(k), axis=0, keepdims=True) + RMS_EPS))
    return jnp.stack(taps)


def _hyena_core(p, conv_w, conv_b, f_w1, f_b1, f_w2, f_b2, f_w3, f_freq, f_bias):
    L = p.shape[1]
    vx = short_conv(p, conv_w, conv_b)
    k2 = _hyena_filter_taps(L, f_w1, f_b1, f_w2, f_b2, f_w3, f_freq)
    return _hyena_long_convs(vx, k2, f_bias)


def _axial_rope_tables(L):
    rows = L // GRID_W
    row = jnp.broadcast_to(jnp.arange(rows, dtype=F32)[:, None], (rows, GRID_W)).reshape(L)
    col = jnp.broadcast_to(jnp.arange(GRID_W, dtype=F32)[None, :], (rows, GRID_W)).reshape(L)
    n_freq = HEAD_DIM // 4
    inv_freq = ROPE_THETA ** (-jnp.arange(n_freq, dtype=F32) / n_freq)
    ang = jnp.concatenate([row[:, None] * inv_freq, col[:, None] * inv_freq], axis=-1)
    return jnp.cos(ang), jnp.sin(ang)


def _expert_choice_ffn(x, shift, scale, w_router_t, w_gate, w_up, w_down, layer):
    B, n, D = x.shape
    cap = CAPACITY_FACTOR * n // N_EXPERTS
    aff = router(x, shift, scale, w_router_t)
    gate, idx = lax.top_k(aff, cap)
    xg = gather_rows(x, shift, scale, idx).reshape(N_EXPERTS, B * cap, D)
    gt = jnp.swapaxes(gate, 0, 1).reshape(N_EXPERTS, B * cap, 1)
    y = expert_ffn(xg, w_gate, w_up, w_down, layer, gt)
    return scatter_add_rows(y, idx, n)


def kernel(x, c, ctx, c_ctx, mod_w, mod_b, ln_g, ln_b, hy_w_in, hy_conv_w, hy_conv_b, hy_f_w1, hy_f_b1, hy_f_w2, hy_f_b2, hy_f_w3, hy_f_freq, hy_f_bias, hy_w_out, at_w_qkv, at_q_gain, at_k_gain, at_w_out, cm_w_in, cm_ln_g, cm_ln_b, cm_w_s, cm_b_s, cm_w_out, moe_router, moe_w_gate, moe_w_up, moe_w_down):
    B, L, D = x.shape
    Lc = ctx.shape[1]
    x_lat, x_ctx = x, ctx
    for i in range(DEPTH):
        kind, slot = LAYER_KINDS[i], KIND_SLOT[i]
        ctx_read = i <= LAST_CTX_LAYER
        ctx_update = i < LAST_CTX_LAYER
        sh1, sc1, g1, sh2, sc2, g2 = _adaln(c[:, None, :], mod_w[i], mod_b[i])
        if ctx_read:
            cmod = [jnp.broadcast_to(m[None, None, :], (B, 1, D)) for m in _adaln(c_ctx, mod_w[i], mod_b[i])]
        lg0, lb0, lg1, lb1 = ln_g[i, 0][None], ln_b[i, 0][None], ln_g[i, 1][None], ln_b[i, 1][None]

        if kind == 0:
            w_in = hy_w_in[slot].astype(BF16)
            w_out = hy_w_out[slot].astype(BF16)
            hy = (hy_conv_w[slot], hy_conv_b[slot], hy_f_w1[slot], hy_f_b1[slot], hy_f_w2[slot], hy_f_b2[slot],
                  hy_f_w3[slot], hy_f_freq[slot], hy_f_bias[slot])
            z = _hyena_core(mod_matmul(x_lat, sh1, sc1, w_in), *hy)
            x_lat = matmul_postnorm(z, w_out, x_lat, g1, lg0, lb0)
            if ctx_update:
                zc = _hyena_core(mod_matmul(x_ctx, cmod[0], cmod[1], w_in), *hy)
                x_ctx = matmul_postnorm(zc, w_out, x_ctx, cmod[2], lg0, lb0)
        elif kind == 1:
            w_qkv = at_w_qkv[slot].astype(BF16)
            w_out = at_w_out[slot].astype(BF16).reshape(N_Q_HEADS, HEAD_DIM, D)
            cos, sin = _axial_rope_tables(L)
            cos_t = jnp.tile(cos, (1, 128 // cos.shape[1]))
            sin_t = jnp.tile(jnp.concatenate([-sin, sin], axis=-1), (1, 128 // HEAD_DIM))
            qkv = mod_matmul(x_lat, sh1, sc1, w_qkv)
            q, k, v = qkv_prep(qkv, at_q_gain[slot], at_k_gain[slot], N_Q_HEADS, N_KV_HEADS, (cos_t, sin_t))
            kvc = mod_matmul(x_ctx, cmod[0], cmod[1], w_qkv[:, Q_COLS:])
            ck, cv = qkv_prep(kvc, None, at_k_gain[slot], 0, N_KV_HEADS, None)
            o = attention(q, ck, cv, k, v)
            x_lat = heads_matmul_postnorm(o, w_out, x_lat, g1, lg0, lb0)
        else:
            w_in = cm_w_in[slot].astype(BF16)
            w_out = cm_w_out[slot].astype(BF16)
            x_lat = gmlp_block(x_lat, sh1, sc1, w_in, cm_ln_g[slot][None], cm_ln_b[slot][None],
                               cm_w_s[slot].astype(BF16), cm_b_s[slot].T, w_out, g1, lg0, lb0)

        moe = (moe_router[i].T.astype(BF16), moe_w_gate, moe_w_up, moe_w_down, i)
        x_lat = postnorm(_expert_choice_ffn(x_lat, sh2, sc2, *moe), x_lat, g2, lg1, lb1)
        if ctx_update:
            x_ctx = postnorm(_expert_choice_ffn(x_ctx, cmod[3], cmod[4], *moe), x_ctx, cmod[5], lg1, lb1)
    return x_lat
```

```python
import math
from functools import partial

import jax
import jax.numpy as jnp
from jax import lax
from jax.experimental import pallas as pl
from jax.experimental.pallas import tpu as pltpu

F32 = jnp.float32
BF16 = jnp.bfloat16

D_MODEL = 1024
DEPTH = 4
GRID_W = 64
N_MIXERS = 3
LAYER_KINDS = tuple(i % N_MIXERS for i in range(DEPTH))
KIND_SLOT = tuple(LAYER_KINDS[:i].count(LAYER_KINDS[i]) for i in range(DEPTH))
LAST_CTX_LAYER = max([i for i, k in enumerate(LAYER_KINDS) if k == 1], default=-1)

HY_ORDER = 2
HY_BANDS = 16
HY_DECAY_TARGET = 1e-2
HY_FAST_DECAY_PCT = 0.3
HY_SLOW_DECAY_PCT = 1.5
HY_MIN_DECAY = math.log(HY_DECAY_TARGET) / HY_SLOW_DECAY_PCT
HY_MAX_DECAY = math.log(HY_DECAY_TARGET) / HY_FAST_DECAY_PCT

HEAD_DIM = 64
V_EXT = 128
N_Q_HEADS = D_MODEL // HEAD_DIM
N_KV_HEADS = 4
GQA_GROUP = N_Q_HEADS // N_KV_HEADS
ROPE_THETA = 10000.0
ATTN_SCALE = HEAD_DIM ** -0.5
Q_COLS = N_Q_HEADS * HEAD_DIM

CHUNK = 128
GMLP_HALF = 2 * D_MODEL
GMLP_GROUPS = 16
GMLP_GROUP_W = GMLP_HALF // GMLP_GROUPS

N_EXPERTS = 16
CAPACITY_FACTOR = 2

RMS_EPS = 1e-6
LN_EPS = 1e-5
DEEPNORM_ALPHA = (2 * DEPTH) ** 0.25

VMEM_LIMIT_BYTES = 56 * 1024 * 1024
OUT_TILE_BYTES = 4 * 1024 * 1024


def _row_tile(rows, n_cols):
    tm = 512
    while tm > 8 and (tm * n_cols * 4 > OUT_TILE_BYTES or rows % tm):
        tm //= 2
    return tm


def _params(sem):
    return pltpu.CompilerParams(dimension_semantics=sem, vmem_limit_bytes=VMEM_LIMIT_BYTES)


def _gelu_erf(x):
    return 0.5 * x * (1.0 + lax.erf(x * (2.0 ** -0.5)))


def _mod_matmul_kernel(x_ref, sh_ref, sc_ref, w_ref, o_ref):
    h = x_ref[0] * (1.0 + sc_ref[0]) + sh_ref[0]
    o_ref[0] = jnp.dot(h.astype(BF16), w_ref[...], preferred_element_type=F32)


def mod_matmul(x, shift, scale, w):
    B, L, K = x.shape
    N = w.shape[1]
    tm = _row_tile(L, N)
    return pl.pallas_call(
        _mod_matmul_kernel,
        grid=(B, L // tm),
        in_specs=[
            pl.BlockSpec((1, tm, K), lambda b, i: (b, i, 0)),
            pl.BlockSpec((1, 1, K), lambda b, i: (b, 0, 0)),
            pl.BlockSpec((1, 1, K), lambda b, i: (b, 0, 0)),
            pl.BlockSpec((K, N), lambda b, i: (0, 0)),
        ],
        out_specs=pl.BlockSpec((1, tm, N), lambda b, i: (b, i, 0)),
        out_shape=jax.ShapeDtypeStruct((B, L, N), F32),
        compiler_params=_params(("parallel", "parallel")),
        name="mod_matmul",
    )(x, shift, scale, w)


def _layer_norm_rows(t, g, b):
    mu = jnp.mean(t, axis=-1, keepdims=True)
    d = t - mu
    var = jnp.mean(d * d, axis=-1, keepdims=True)
    return d * lax.rsqrt(var + LN_EPS) * g + b


def _matmul_postnorm_kernel(z_ref, w_ref, x_ref, gate_ref, g_ref, b_ref, o_ref):
    y = jnp.dot(z_ref[0].astype(BF16), w_ref[...], preferred_element_type=F32)
    t = DEEPNORM_ALPHA * x_ref[0] + gate_ref[0] * y
    o_ref[0] = _layer_norm_rows(t, g_ref[...], b_ref[...])


def matmul_postnorm(z, w, x, gate, g, b):
    B, L, K = z.shape
    D = w.shape[1]
    tm = _row_tile(L, max(K, D))
    return pl.pallas_call(
        _matmul_postnorm_kernel,
        grid=(B, L // tm),
        in_specs=[
            pl.BlockSpec((1, tm, K), lambda bi, i: (bi, i, 0)),
            pl.BlockSpec((K, D), lambda bi, i: (0, 0)),
            pl.BlockSpec((1, tm, D), lambda bi, i: (bi, i, 0)),
            pl.BlockSpec((1, 1, D), lambda bi, i: (bi, 0, 0)),
            pl.BlockSpec((1, D), lambda bi, i: (0, 0)),
            pl.BlockSpec((1, D), lambda bi, i: (0, 0)),
        ],
        out_specs=pl.BlockSpec((1, tm, D), lambda bi, i: (bi, i, 0)),
        out_shape=jax.ShapeDtypeStruct((B, L, D), F32),
        compiler_params=_params(("parallel", "parallel")),
        name="matmul_postnorm",
    )(z, w, x, gate, g, b)


def _postnorm_kernel(y_ref, x_ref, gate_ref, g_ref, b_ref, o_ref):
    t = DEEPNORM_ALPHA * x_ref[0] + gate_ref[0] * y_ref[0]
    o_ref[0] = _layer_norm_rows(t, g_ref[...], b_ref[...])


def postnorm(y, x, gate, g, b):
    B, L, D = x.shape
    tm = _row_tile(L, D)
    return pl.pallas_call(
        _postnorm_kernel,
        grid=(B, L // tm),
        in_specs=[
            pl.BlockSpec((1, tm, D), lambda bi, i: (bi, i, 0)),
            pl.BlockSpec((1, tm, D), lambda bi, i: (bi, i, 0)),
            pl.BlockSpec((1, 1, D), lambda bi, i: (bi, 0, 0)),
            pl.BlockSpec((1, D), lambda bi, i: (0, 0)),
            pl.BlockSpec((1, D), lambda bi, i: (0, 0)),
        ],
        out_specs=pl.BlockSpec((1, tm, D), lambda bi, i: (bi, i, 0)),
        out_shape=jax.ShapeDtypeStruct((B, L, D), F32),
        compiler_params=_params(("parallel", "parallel")),
        name="postnorm",
    )(y, x, gate, g, b)


def _gmlp_block_kernel(x_ref, sh_ref, sc_ref, win_ref, lng_ref, lnb_ref, ws_ref, bs_ref, w_ref, gate_ref, g_ref, b_ref,
                       o_ref):
    h = (x_ref[0] * (1.0 + sc_ref[0]) + sh_ref[0]).astype(BF16)
    z = _gelu_erf(jnp.dot(h, win_ref[...], preferred_element_type=F32))
    tm, half = z.shape[0], z.shape[1] // 2
    u, v = z[:, :half], z[:, half:]
    vn = _layer_norm_rows(v, lng_ref[...], lnb_ref[...]).astype(BF16)
    n_grp = ws_ref.shape[0]
    gw = half // n_grp
    chunks = []
    for c in range(tm // CHUNK):
        cols = []
        for g in range(n_grp):
            blk = vn[c * CHUNK:(c + 1) * CHUNK, g * gw:(g + 1) * gw]
            cols.append(jnp.dot(ws_ref[g], blk, preferred_element_type=F32) + bs_ref[:, g:g + 1])
        chunks.append(jnp.concatenate(cols, axis=1))
    sv = jnp.concatenate(chunks, axis=0)
    y = jnp.dot((u * sv).astype(BF16), w_ref[...], preferred_element_type=F32)
    t = DEEPNORM_ALPHA * x_ref[0] + gate_ref[0] * y
    o_ref[0] = _layer_norm_rows(t, g_ref[...], b_ref[...])


def gmlp_block(x, shift, scale, w_in, ln_g, ln_b, w_s, b_s_t, w_out, gate, g, b, tm=256):
    B, L, D = x.shape
    H = w_out.shape[0]
    const = lambda shape: pl.BlockSpec(shape, lambda bi, i: (0,) * len(shape))
    vec = pl.BlockSpec((1, 1, D), lambda bi, i: (bi, 0, 0))
    return pl.pallas_call(
        _gmlp_block_kernel,
        grid=(B, L // tm),
        in_specs=[
            pl.BlockSpec((1, tm, D), lambda bi, i: (bi, i, 0)), vec, vec, const((D, 2 * H)),
            const((1, H)), const((1, H)), const(w_s.shape), const(b_s_t.shape), const((H, D)),
            vec, const((1, D)), const((1, D)),
        ],
        out_specs=pl.BlockSpec((1, tm, D), lambda bi, i: (bi, i, 0)),
        out_shape=jax.ShapeDtypeStruct((B, L, D), F32),
        compiler_params=_params(("parallel", "parallel")),
        name="gmlp_block",
    )(x, shift, scale, w_in, ln_g, ln_b, w_s, b_s_t, w_out, gate, g, b)


def _head_rms_norm(x, ones_bd, gain):
    sq = x * x
    sq_hi = sq.astype(BF16)
    sq_lo = (sq - sq_hi.astype(F32)).astype(BF16)
    ss = jnp.dot(sq_hi, ones_bd, preferred_element_type=F32) + jnp.dot(sq_lo, ones_bd, preferred_element_type=F32)
    return x * lax.rsqrt(ss * (1.0 / HEAD_DIM) + RMS_EPS) * gain


def _rope(x, cos_t, sin_t):
    w = x.shape[1]
    lane = lax.broadcasted_iota(jnp.int32, x.shape, 1)
    half = HEAD_DIM // 2
    partner = jnp.where((lane & (HEAD_DIM - 1)) < half, pltpu.roll(x, w - half, 1), pltpu.roll(x, half, 1))
    reps = w // cos_t.shape[1]
    return x * jnp.tile(cos_t, (1, reps)) + partner * jnp.tile(sin_t, (1, reps))


def _qkv_prep_kernel(*refs, n_q, rope):
    refs = list(refs)
    x_ref = refs.pop(0)
    cos_ref, sin_ref = (refs.pop(0), refs.pop(0)) if rope else (None, None)
    if n_q:
        qg_ref, qbd_ref = refs.pop(0), refs.pop(0)
    kg_ref, kbd_ref = refs.pop(0), refs.pop(0)
    if n_q:
        q_out = refs.pop(0)
    k_out, v_out = refs
    x = x_ref[0]
    n_kv = k_out.shape[1]
    qc, kc = n_q * HEAD_DIM, n_kv * HEAD_DIM
    if n_q:
        q = _head_rms_norm(x[:, :qc], qbd_ref[...], qg_ref[...])
        if rope:
            q = _rope(q, cos_ref[...], sin_ref[...])
        q = q * ATTN_SCALE
        for h in range(n_q):
            q_out[0, h] = q[:, h * HEAD_DIM:(h + 1) * HEAD_DIM].astype(BF16)
    k = _head_rms_norm(x[:, qc:qc + kc], kbd_ref[...], kg_ref[...])
    if rope:
        k = _rope(k, cos_ref[...], sin_ref[...])
    lane = lax.broadcasted_iota(jnp.int32, (x.shape[0], V_EXT), 1)
    for h in range(n_kv):
        k_out[0, h] = k[:, h * HEAD_DIM:(h + 1) * HEAD_DIM].astype(BF16)
        c0 = qc + kc + h * HEAD_DIM
        base = c0 // V_EXT * V_EXT
        vv = x[:, base:base + V_EXT]
        if c0 != base:
            vv = pltpu.roll(vv, V_EXT - (c0 - base), 1)
        v_out[0, h] = jnp.where(lane < HEAD_DIM, vv, (lane == HEAD_DIM).astype(F32)).astype(BF16)


def qkv_prep(x, q_gain, k_gain, n_q, n_kv, rope_tables, tm=256):
    B, L, C = x.shape
    hd = HEAD_DIM
    tm = min(tm, L)

    def ones_bd(n):
        r = jnp.arange(n * hd) // hd
        return (r[:, None] == r[None, :]).astype(BF16)

    const = lambda shape: pl.BlockSpec(shape, lambda b, i: (0,) * len(shape))
    args, specs = [x], [pl.BlockSpec((1, tm, C), lambda b, i: (b, i, 0))]
    if rope_tables is not None:
        args += list(rope_tables)
        specs += [pl.BlockSpec((tm, 128), lambda b, i: (i, 0))] * 2
    if n_q:
        args += [jnp.tile(q_gain, n_q)[None], ones_bd(n_q)]
        specs += [const((1, n_q * hd)), const((n_q * hd, n_q * hd))]
    args += [jnp.tile(k_gain, n_kv)[None], ones_bd(n_kv)]
    specs += [const((1, n_kv * hd)), const((n_kv * hd, n_kv * hd))]
    head_spec = lambda n, w=hd: pl.BlockSpec((1, n, tm, w), lambda b, i: (b, 0, i, 0))
    head_shape = lambda n, w=hd: jax.ShapeDtypeStruct((B, n, L, w), BF16)
    out_specs = ([head_spec(n_q)] if n_q else []) + [head_spec(n_kv), head_spec(n_kv, V_EXT)]
    out_shape = ([head_shape(n_q)] if n_q else []) + [head_shape(n_kv), head_shape(n_kv, V_EXT)]
    return pl.pallas_call(
        partial(_qkv_prep_kernel, n_q=n_q, rope=rope_tables is not None),
        grid=(B, L // tm),
        in_specs=specs,
        out_specs=out_specs,
        out_shape=out_shape,
        compiler_params=_params(("parallel", "parallel")),
        name="qkv_prep",
    )(*args)


def _nt_dot(a, b):
    return lax.dot_general(a, b, (((1,), (1,)), ((), ())), preferred_element_type=F32)


def _attention_kernel(q_ref, ck_ref, cv_ref, k_ref, v_ref, o_ref):
    g, hd = q_ref.shape[1], q_ref.shape[3]
    for h in range(g):
        q = q_ref[0, h]
        s_c = _nt_dot(q, ck_ref[0, 0])
        s_l = _nt_dot(q, k_ref[0, 0])
        m = jnp.maximum(jnp.max(s_c, axis=-1, keepdims=True), jnp.max(s_l, axis=-1, keepdims=True))
        p_c = jnp.exp(s_c - m).astype(BF16)
        p_l = jnp.exp(s_l - m).astype(BF16)
        o = (jnp.dot(p_c, cv_ref[0, 0], preferred_element_type=F32)
             + jnp.dot(p_l, v_ref[0, 0], preferred_element_type=F32))
        o_ref[0, h] = (o[:, :hd] / o[:, hd:hd + 1]).astype(BF16)


def attention(q, ck, cv, k, v, tq=256):
    B, H, L, hd = q.shape
    KV, Lc = ck.shape[1], ck.shape[2]
    G = H // KV
    kv_spec = lambda n, w=hd: pl.BlockSpec((1, 1, n, w), lambda b, kv, i: (b, kv, 0, 0))
    q_spec = pl.BlockSpec((1, G, tq, hd), lambda b, kv, i: (b, kv, i, 0))
    return pl.pallas_call(
        _attention_kernel,
        grid=(B, KV, L // tq),
        in_specs=[q_spec, kv_spec(Lc), kv_spec(Lc, V_EXT), kv_spec(L), kv_spec(L, V_EXT)],
        out_specs=q_spec,
        out_shape=jax.ShapeDtypeStruct((B, H, L, hd), BF16),
        compiler_params=_params(("parallel", "parallel", "parallel")),
        name="gqa_attention",
    )(q, ck, cv, k, v)


def _heads_matmul_postnorm_kernel(o_ref, w_ref, x_ref, gate_ref, g_ref, b_ref, out_ref):
    y = jnp.dot(o_ref[0, 0], w_ref[0], preferred_element_type=F32)
    for h in range(1, o_ref.shape[1]):
        y += jnp.dot(o_ref[0, h], w_ref[h], preferred_element_type=F32)
    t = DEEPNORM_ALPHA * x_ref[0] + gate_ref[0] * y
    out_ref[0] = _layer_norm_rows(t, g_ref[...], b_ref[...])


def heads_matmul_postnorm(o, w, x, gate, g, b, tm=512):
    B, H, L, hd = o.shape
    D = w.shape[2]
    return pl.pallas_call(
        _heads_matmul_postnorm_kernel,
        grid=(B, L // tm),
        in_specs=[
            pl.BlockSpec((1, H, tm, hd), lambda bi, i: (bi, 0, i, 0)),
            pl.BlockSpec((H, hd, D), lambda bi, i: (0, 0, 0)),
            pl.BlockSpec((1, tm, D), lambda bi, i: (bi, i, 0)),
            pl.BlockSpec((1, 1, D), lambda bi, i: (bi, 0, 0)),
            pl.BlockSpec((1, D), lambda bi, i: (0, 0)),
            pl.BlockSpec((1, D), lambda bi, i: (0, 0)),
        ],
        out_specs=pl.BlockSpec((1, tm, D), lambda bi, i: (bi, i, 0)),
        out_shape=jax.ShapeDtypeStruct((B, L, D), F32),
        compiler_params=_params(("parallel", "parallel")),
        name="heads_matmul_postnorm",
    )(o, w, x, gate, g, b)


def _router_kernel(x_ref, sh_ref, sc_ref, w_ref, o_ref):
    h = (x_ref[0] * (1.0 + sc_ref[0]) + sh_ref[0]).astype(BF16)
    logits = _nt_dot(w_ref[...], h)
    e = jnp.exp(logits - jnp.max(logits, axis=0, keepdims=True))
    o_ref[0] = e / jnp.sum(e, axis=0, keepdims=True)


def router(x, shift, scale, w_t):
    B, n, D = x.shape
    E = w_t.shape[0]
    tm = min(n, 512)
    vec = pl.BlockSpec((1, 1, D), lambda b, i: (b, 0, 0))
    return pl.pallas_call(
        _router_kernel,
        grid=(B, n // tm),
        in_specs=[pl.BlockSpec((1, tm, D), lambda b, i: (b, i, 0)), vec, vec, pl.BlockSpec((E, D), lambda b, i: (0, 0))],
        out_specs=pl.BlockSpec((1, E, tm), lambda b, i: (b, 0, i)),
        out_shape=jax.ShapeDtypeStruct((B, E, n), F32),
        compiler_params=_params(("parallel", "parallel")),
        name="moe_router",
    )(x, shift, scale, w_t)


ROUTE_UNROLL = 8


def _gather_rows_kernel(idx_ref, x_ref, sh_ref, sc_ref, o_ref, buf):
    cap = buf.shape[0]
    base = (pl.program_id(0) * pl.num_programs(1) + pl.program_id(1)) * cap

    def body(s, carry):
        for u in range(ROUTE_UNROLL):
            r = s * ROUTE_UNROLL + u
            buf[pl.ds(r, 1), :] = x_ref[pl.ds(idx_ref[base + r], 1), :]
        return carry

    lax.fori_loop(0, cap // ROUTE_UNROLL, body, 0)
    o_ref[...] = (buf[...] * (1.0 + sc_ref[...]) + sh_ref[...]).astype(BF16)


def gather_rows(x, shift, scale, idx):
    B, n, D = x.shape
    _, E, cap = idx.shape
    vec = pl.BlockSpec((None, 1, D), lambda b, e, idx: (b, 0, 0))
    return pl.pallas_call(
        _gather_rows_kernel,
        grid_spec=pltpu.PrefetchScalarGridSpec(
            num_scalar_prefetch=1,
            grid=(B, E),
            in_specs=[pl.BlockSpec((None, n, D), lambda b, e, idx: (b, 0, 0)), vec, vec],
            out_specs=pl.BlockSpec((None, None, cap, D), lambda b, e, idx: (e, b, 0, 0)),
            scratch_shapes=[pltpu.VMEM((cap, D), F32)],
        ),
        out_shape=jax.ShapeDtypeStruct((E, B, cap, D), BF16),
        compiler_params=_params(("parallel", "arbitrary")),
        name="moe_gather_rows",
    )(idx.reshape(-1), x, shift, scale)


def _scatter_add_kernel(idx_ref, y_ref, o_ref):
    cap = y_ref.shape[0]
    base = (pl.program_id(0) * pl.num_programs(1) + pl.program_id(1)) * cap

    @pl.when(pl.program_id(1) == 0)
    def _():
        o_ref[...] = jnp.zeros(o_ref.shape, o_ref.dtype)

    def body(s, carry):
        rows = [idx_ref[base + s * ROUTE_UNROLL + u] for u in range(ROUTE_UNROLL)]
        sums = [o_ref[pl.ds(rows[u], 1), :] + y_ref[pl.ds(s * ROUTE_UNROLL + u, 1), :] for u in range(ROUTE_UNROLL)]
        for u in range(ROUTE_UNROLL):
            o_ref[pl.ds(rows[u], 1), :] = sums[u]
        return carry

    lax.fori_loop(0, cap // ROUTE_UNROLL, body, 0)


def scatter_add_rows(y, idx, n):
    E, R, D = y.shape
    B, _, cap = idx.shape
    return pl.pallas_call(
        _scatter_add_kernel,
        grid_spec=pltpu.PrefetchScalarGridSpec(
            num_scalar_prefetch=1,
            grid=(B, E),
            in_specs=[pl.BlockSpec((None, cap, D), lambda b, e, idx: (e, b, 0))],
            out_specs=pl.BlockSpec((None, n, D), lambda b, e, idx: (b, 0, 0)),
        ),
        out_shape=jax.ShapeDtypeStruct((B, n, D), F32),
        compiler_params=_params(("parallel", "arbitrary")),
        name="moe_scatter_add_rows",
    )(idx.reshape(-1), y)


FFN_ROW_SPLIT = 2


def _expert_ffn_kernel(x_ref, wg_ref, wu_ref, wd_ref, gate_ref, o_ref):
    f = pl.program_id(1)
    wg, wu, wd = wg_ref[0].astype(BF16), wu_ref[0].astype(BF16), wd_ref[0].astype(BF16)
    rows = x_ref.shape[1] // FFN_ROW_SPLIT
    for r in range(FFN_ROW_SPLIT):
        rs = slice(r * rows, (r + 1) * rows)
        x = x_ref[0, rs]
        a = jnp.dot(x, wg, preferred_element_type=F32)
        u = jnp.dot(x, wu, preferred_element_type=F32)
        h = (a * jax.nn.sigmoid(a) * u).astype(BF16)
        y = jnp.dot(h, wd, preferred_element_type=F32)

        @pl.when(f == 0)
        def _():
            o_ref[0, rs] = y

        @pl.when(f > 0)
        def _():
            o_ref[0, rs] += y

    @pl.when(f == pl.num_programs(1) - 1)
    def _():
        o_ref[0] *= gate_ref[0]


def expert_ffn(xg, w_gate, w_up, w_down, layer, gate, tf=512):
    E, R, D = xg.shape
    F = w_gate.shape[3]
    return pl.pallas_call(
        _expert_ffn_kernel,
        grid=(E, F // tf),
        in_specs=[
            pl.BlockSpec((1, R, D), lambda e, f: (e, 0, 0)),
            pl.BlockSpec((None, 1, D, tf), lambda e, f: (layer, e, 0, f)),
            pl.BlockSpec((None, 1, D, tf), lambda e, f: (layer, e, 0, f)),
            pl.BlockSpec((None, 1, tf, D), lambda e, f: (layer, e, f, 0)),
            pl.BlockSpec((1, R, 1), lambda e, f: (e, 0, 0)),
        ],
        out_specs=pl.BlockSpec((1, R, D), lambda e, f: (e, 0, 0)),
        out_shape=jax.ShapeDtypeStruct((E, R, D), F32),
        compiler_params=_params(("parallel", "arbitrary")),
        name="expert_ffn",
    )(xg, w_gate, w_up, w_down, gate)


def _short_conv_kernel(p_ref, w_ref, b_ref, o_ref):
    q = p_ref[0]
    L = q.shape[0]
    row = lax.broadcasted_iota(jnp.int32, q.shape, 0)
    prev = jnp.where(row == 0, 0.0, pltpu.roll(q, 1, 0))
    nxt = jnp.where(row == L - 1, 0.0, pltpu.roll(q, L - 1, 0))
    w = w_ref[...]
    o_ref[0, 0] = prev * w[0:1] + q * w[1:2] + nxt * w[2:3] + b_ref[...]


def short_conv(p, conv_w, conv_b, tc=256):
    B, L, D3 = p.shape
    D = D3 // 3
    nj = D // tc
    return pl.pallas_call(
        _short_conv_kernel,
        grid=(3, B, nj),
        in_specs=[
            pl.BlockSpec((1, L, tc), lambda s, b, j: (b, 0, s * nj + j)),
            pl.BlockSpec((3, tc), lambda s, b, j: (0, s * nj + j)),
            pl.BlockSpec((1, tc), lambda s, b, j: (0, s * nj + j)),
        ],
        out_specs=pl.BlockSpec((1, 1, L, tc), lambda s, b, j: (s, b, 0, j)),
        out_shape=jax.ShapeDtypeStruct((3, B, L, D), F32),
        compiler_params=_params(("parallel", "parallel", "parallel")),
        name="hyena_short_conv",
    )(p, conv_w, conv_b[None])


FFT_GROUP = 128
SUBLANES = 8
FFT_DATA_PASSES = 1
FFT_FILTER_PASSES = 3


def _split_hi_lo(m):
    hi = m.astype(BF16)
    return hi, (m - hi.astype(F32)).astype(BF16)


def _dot_split(m_hi, m_lo, x, passes):
    x_hi = x.astype(BF16)
    acc = jnp.dot(m_hi, x_hi, preferred_element_type=F32)
    if passes >= 2:
        x_lo = (x - x_hi.astype(F32)).astype(BF16)
        acc += jnp.dot(m_hi, x_lo, preferred_element_type=F32)
    if passes >= 3:
        acc += jnp.dot(m_lo, x_hi, preferred_element_type=F32)
    return acc


def _cos_sin(num, den):
    ang = (2.0 * math.pi / den) * (num % den).astype(F32)
    return jnp.cos(ang), jnp.sin(ang)


def _outer_tables(N1, n_in):
    k1 = jnp.arange(N1, dtype=jnp.int32)
    c, s = _cos_sin(k1[:, None] * jnp.arange(n_in, dtype=jnp.int32)[None, :], N1)
    fwd = jnp.concatenate([jnp.concatenate([c, s], 1), jnp.concatenate([-s, c], 1)], 0)
    cf, sf = _cos_sin(k1[:, None] * k1[None, :], N1)
    fwd_real = jnp.concatenate([cf, -sf], 0)
    ci, si = c.T, s.T
    inv = jnp.concatenate([jnp.concatenate([ci, -si], 1), jnp.concatenate([si, ci], 1)], 0) / (N1 * FFT_GROUP)
    return fwd, fwd_real, inv


def _group_tables(N1):
    G = FFT_GROUP
    k = jnp.arange(N1, dtype=jnp.int32)[:, None, None] + N1 * jnp.arange(G, dtype=jnp.int32)[None, :, None]
    gr, gs = _cos_sin(k * jnp.arange(G, dtype=jnp.int32)[None, None, :], N1 * G)
    gi = -gs
    fwd = jnp.concatenate([jnp.concatenate([gr, -gi], 2), jnp.concatenate([gi, gr], 2)], 1)
    return fwd, jnp.swapaxes(fwd, 1, 2)


def _tile_expand(m):
    r, c = m.shape
    eye = jnp.eye(SUBLANES, dtype=m.dtype)
    return (m[:, None, :, None] * eye[None, :, None, :]).reshape(r * SUBLANES, c * SUBLANES)


def _outer_fwd_kernel(z_ref, m_hi, m_lo, o_ref, *, passes):
    parts, n_in, sub, d = z_ref.shape
    x = z_ref[...].reshape(parts * n_in * sub, d)
    r = _dot_split(m_hi[...], m_lo[...], x, passes)
    o_ref[0] = r.reshape(2, r.shape[0] // (2 * sub), sub, d)


def outer_fwd(z, m, parts, first, n_seq):
    _, n_in, G, D = z.shape
    N1 = m.shape[0] // 2
    m_hi, m_lo = _split_hi_lo(_tile_expand(m))
    first_blk = first // parts
    return pl.pallas_call(
        partial(_outer_fwd_kernel, passes=FFT_DATA_PASSES if parts == 2 else FFT_FILTER_PASSES),
        grid=(n_seq, G // SUBLANES),
        in_specs=[
            pl.BlockSpec((parts, n_in, SUBLANES, D), lambda p, j: (first_blk + p, 0, j, 0)),
            pl.BlockSpec(m_hi.shape, lambda p, j: (0, 0)),
            pl.BlockSpec(m_hi.shape, lambda p, j: (0, 0)),
        ],
        out_specs=pl.BlockSpec((1, 2, N1, SUBLANES, D), lambda p, j: (p, 0, 0, j, 0)),
        out_shape=jax.ShapeDtypeStruct((n_seq, 2, N1, G, D), F32),
        compiler_params=_params(("parallel", "parallel")),
        name="hyena_dft_outer_fwd",
    )(z, m_hi, m_lo)


def _group_spectrum_kernel(a_ref, g_hi, g_lo, o_ref):
    a = a_ref[0]
    x = _dot_split(g_hi[0], g_lo[0], a.reshape(a.shape[0] * a.shape[1], a.shape[2]), FFT_FILTER_PASSES)
    o_ref[...] = x.reshape(2, x.shape[0] // 2, x.shape[1])


def group_spectrum(a, g):
    _, parts, rows, D = a.shape
    n_grp, Rm2 = g.shape[0], g.shape[1]
    Rin = rows // n_grp
    g_hi, g_lo = _split_hi_lo(g)
    return pl.pallas_call(
        _group_spectrum_kernel,
        grid=(n_grp,),
        in_specs=[
            pl.BlockSpec((1, parts, Rin, D), lambda k: (0, 0, k, 0)),
            pl.BlockSpec((1,) + g.shape[1:], lambda k: (k, 0, 0)),
            pl.BlockSpec((1,) + g.shape[1:], lambda k: (k, 0, 0)),
        ],
        out_specs=pl.BlockSpec((2, Rm2 // 2, D), lambda k: (0, k, 0)),
        out_shape=jax.ShapeDtypeStruct((2, n_grp * Rm2 // 2, D), F32),
        compiler_params=_params(("parallel",)),
        name="hyena_filter_spectrum",
    )(a, g_hi, g_lo)


def _group_conv_kernel(a_ref, kf_ref, g_hi, g_lo, h_hi, h_lo, o_ref):
    n_grp = g_hi.shape[0]
    r_in, r_mid, r_out = a_ref.shape[2] // n_grp, kf_ref.shape[1] // n_grp, o_ref.shape[2] // n_grp
    for j in range(n_grp):
        a = jnp.concatenate([a_ref[0, 0, j * r_in:(j + 1) * r_in], a_ref[0, 1, j * r_in:(j + 1) * r_in]], axis=0)
        x = _dot_split(g_hi[j], g_lo[j], a, FFT_DATA_PASSES)
        xr, xi = x[:r_mid], x[r_mid:]
        kr, ki = kf_ref[0, j * r_mid:(j + 1) * r_mid], kf_ref[1, j * r_mid:(j + 1) * r_mid]
        y = jnp.concatenate([xr * kr - xi * ki, xr * ki + xi * kr], axis=0)
        p = _dot_split(h_hi[j], h_lo[j], y, FFT_DATA_PASSES)
        o_ref[0, 0, j * r_out:(j + 1) * r_out] = p[:r_out]
        o_ref[0, 1, j * r_out:(j + 1) * r_out] = p[r_out:]


def group_conv(a, kf, g, h):
    P, _, rows, D = a.shape
    n_grp, Rm2, Rout2 = g.shape[0], g.shape[1], h.shape[1]
    Rin = rows // n_grp
    gs = 2 if n_grp % 2 == 0 else 1
    g_hi, g_lo = _split_hi_lo(g)
    h_hi, h_lo = _split_hi_lo(h)
    gspec = pl.BlockSpec((gs,) + g.shape[1:], lambda k, p: (k, 0, 0))
    hspec = pl.BlockSpec((gs,) + h.shape[1:], lambda k, p: (k, 0, 0))
    return pl.pallas_call(
        _group_conv_kernel,
        grid=(n_grp // gs, P),
        in_specs=[
            pl.BlockSpec((1, 2, gs * Rin, D), lambda k, p: (p, 0, k, 0)),
            pl.BlockSpec((2, gs * Rm2 // 2, D), lambda k, p: (0, k, 0)),
            gspec, gspec, hspec, hspec,
        ],
        out_specs=pl.BlockSpec((1, 2, gs * Rout2 // 2, D), lambda k, p: (p, 0, k, 0)),
        out_shape=jax.ShapeDtypeStruct((P, 2, n_grp * Rout2 // 2, D), F32),
        compiler_params=_params(("parallel", "parallel")),
        name="hyena_group_conv",
    )(a, kf, g_hi, g_lo, h_hi, h_lo)


def _outer_inv_kernel(p_ref, m_hi, m_lo, gate_ref, u_ref, bias_ref, o_ref):
    _, _, n1, sub, d = p_ref.shape
    y = _dot_split(m_hi[...], m_lo[...], p_ref[0].reshape(2 * n1 * sub, d), FFT_DATA_PASSES)
    y = y.reshape(2, y.shape[0] // (2 * sub), sub, d)
    o_ref[...] = gate_ref[...] * (y + u_ref[...] * bias_ref[...])


def outer_inv(pm, m, gate, gate_first, u, u_first, bias):
    P, _, N1, G, D = pm.shape
    n_out = m.shape[0] // 2
    m_hi, m_lo = _split_hi_lo(_tile_expand(m))
    seq = lambda first: pl.BlockSpec((2, n_out, SUBLANES, D), lambda p, j: (first // 2 + p, 0, j, 0))
    return pl.pallas_call(
        _outer_inv_kernel,
        grid=(P, G // SUBLANES),
        in_specs=[
            pl.BlockSpec((1, 2, N1, SUBLANES, D), lambda p, j: (p, 0, 0, j, 0)),
            pl.BlockSpec(m_hi.shape, lambda p, j: (0, 0)),
            pl.BlockSpec(m_hi.shape, lambda p, j: (0, 0)),
            seq(gate_first), seq(u_first),
            pl.BlockSpec((1, D), lambda p, j: (0, 0)),
        ],
        out_specs=seq(0),
        out_shape=jax.ShapeDtypeStruct((2 * P, n_out, G, D), F32),
        compiler_params=_params(("parallel", "parallel")),
        name="hyena_dft_outer_inv",
    )(pm, m_hi, m_lo, gate, u, bias)


def _hyena_long_convs(vx, k2, f_bias):
    _, B, L, D = vx.shape
    N = 2 * L
    G = FFT_GROUP
    if N // G >= 16:
        N1, Nh = N // G, N // G // 2
        fwd, fwd_real, inv = _outer_tables(N1, Nh)
        g_fwd, g_inv = _group_tables(N1)
        vx4 = vx.reshape(3 * B, Nh, G, D)
        k4 = k2.reshape(HY_ORDER, N1, G, D)
        z, z_first = vx4, 0
        for n in range(HY_ORDER):
            kf = group_spectrum(outer_fwd(k4, fwd_real, 1, n, 1).reshape(1, 2, N, D), g_fwd)
            a = outer_fwd(z, fwd, 2, z_first, B // 2).reshape(B // 2, 2, N, D)
            pm = group_conv(a, kf, g_fwd, g_inv).reshape(B // 2, 2, N1, G, D)
            z = outer_inv(pm, inv, vx4, (n + 1) * B, z, z_first, f_bias[n][None])
            z_first = 0
        return z.reshape(B, L, D)
    kk = jnp.arange(N, dtype=jnp.int32)
    c, s = _cos_sin(kk[:, None] * kk[None, :], N)
    g_real = jnp.concatenate([c, -s], 0)[None]
    ch, sh = c[:, :L], s[:, :L]
    g_fwd = jnp.concatenate([jnp.concatenate([ch, sh], 1), jnp.concatenate([-sh, ch], 1)], 0)[None]
    ci, si = ch.T, sh.T
    g_inv = (jnp.concatenate([jnp.concatenate([ci, -si], 1), jnp.concatenate([si, ci], 1)], 0) / N)[None]
    z = vx[0]
    for n in range(HY_ORDER):
        kf = group_spectrum(k2[n].reshape(1, 1, N, D), g_real)
        y = group_conv(z.reshape(B // 2, 2, L, D), kf, g_fwd, g_inv).reshape(B, L, D)
        z = vx[n + 1] * (y + z * f_bias[n])
    return z


def _adaln(cond, w_mod, b_mod):
    m = jax.nn.silu(cond) @ w_mod + b_mod
    return jnp.split(m, 6, axis=-1)


def _hyena_filter_taps(L, f_w1, f_b1, f_w2, f_b2, f_w3, f_freq):
    row = jnp.arange(2 * L, dtype=jnp.int32)
    pos = jnp.where(row < L, row, 2 * L - row)
    t = jnp.linspace(0.0, 1.0, L, dtype=F32)[jnp.minimum(pos, L - 1)][:, None]
    w = (2.0 * math.pi / L) * pos.astype(F32)[:, None]
    f = jnp.linspace(1e-4, HY_BANDS - 1, HY_BANDS, dtype=F32)[None, :]
    z = jnp.concatenate([t, jnp.cos(f * w), -jnp.sin(f * w)], axis=-1)
    h = jnp.sin(f_freq[0] * (z @ f_w1 + f_b1))
    h = jnp.sin(f_freq[1] * (h @ f_w2 + f_b2))
    h = (h @ f_w3).reshape(2 * L, HY_ORDER, 2, D_MODEL)
    deltas = jnp.abs(jnp.linspace(HY_MIN_DECAY, HY_MAX_DECAY, D_MODEL, dtype=F32))
    window = jnp.where(row[:, None] == L, 0.0, jnp.exp(-t * deltas))
    taps = []
    for n in range(HY_ORDER):
        k = jnp.where(row[:, None] < L, h[:, n, 0], h[:, n, 1]) * window
        taps.append(k / (jnp.sum(jnp.abs(k), axis=0, keepdims=True) + RMS_EPS))
    return jnp.stack(taps)


def _hyena_core(p, conv_w, conv_b, f_w1, f_b1, f_w2, f_b2, f_w3, f_freq, f_bias):
    L = p.shape[1]
    vx = short_conv(p, conv_w, conv_b)
    k2 = _hyena_filter_taps(L, f_w1, f_b1, f_w2, f_b2, f_w3, f_freq)
    return _hyena_long_convs(vx, k2, f_bias)


def _axial_rope_tables(L):
    rows = L // GRID_W
    row = jnp.broadcast_to(jnp.arange(rows, dtype=F32)[:, None], (rows, GRID_W)).reshape(L)
    col = jnp.broadcast_to(jnp.arange(GRID_W, dtype=F32)[None, :], (rows, GRID_W)).reshape(L)
    n_freq = HEAD_DIM // 4
    inv_freq = ROPE_THETA ** (-jnp.arange(n_freq, dtype=F32) / n_freq)
    ang = jnp.concatenate([row[:, None] * inv_freq, col[:, None] * inv_freq], axis=-1)
    return jnp.cos(ang), jnp.sin(ang)


def _expert_choice_ffn(x, shift, scale, w_router_t, w_gate, w_up, w_down, layer):
    B, n, D = x.shape
    cap = CAPACITY_FACTOR * n // N_EXPERTS
    aff = router(x, shift, scale, w_router_t)
    gate, idx = lax.top_k(aff, cap)
    xg = gather_rows(x, shift, scale, idx).reshape(N_EXPERTS, B * cap, D)
    gt = jnp.swapaxes(gate, 0, 1).reshape(N_EXPERTS, B * cap, 1)
    y = expert_ffn(xg, w_gate, w_up, w_down, layer, gt)
    return scatter_add_rows(y, idx, n)


def kernel(x, c, ctx, c_ctx, mod_w, mod_b, ln_g, ln_b, hy_w_in, hy_conv_w, hy_conv_b, hy_f_w1, hy_f_b1, hy_f_w2, hy_f_b2, hy_f_w3, hy_f_freq, hy_f_bias, hy_w_out, at_w_qkv, at_q_gain, at_k_gain, at_w_out, cm_w_in, cm_ln_g, cm_ln_b, cm_w_s, cm_b_s, cm_w_out, moe_router, moe_w_gate, moe_w_up, moe_w_down):
    B, L, D = x.shape
    Lc = ctx.shape[1]
    x_lat, x_ctx = x, ctx
    for i in range(DEPTH):
        kind, slot = LAYER_KINDS[i], KIND_SLOT[i]
        ctx_read = i <= LAST_CTX_LAYER
        ctx_update = i < LAST_CTX_LAYER
        sh1, sc1, g1, sh2, sc2, g2 = _adaln(c[:, None, :], mod_w[i], mod_b[i])
        if ctx_read:
            cmod = [jnp.broadcast_to(m[None, None, :], (B, 1, D)) for m in _adaln(c_ctx, mod_w[i], mod_b[i])]
        lg0, lb0, lg1, lb1 = ln_g[i, 0][None], ln_b[i, 0][None], ln_g[i, 1][None], ln_b[i, 1][None]

        if kind == 0:
            w_in = hy_w_in[slot].astype(BF16)
            w_out = hy_w_out[slot].astype(BF16)
            hy = (hy_conv_w[slot], hy_conv_b[slot], hy_f_w1[slot], hy_f_b1[slot], hy_f_w2[slot], hy_f_b2[slot],
                  hy_f_w3[slot], hy_f_freq[slot], hy_f_bias[slot])
            z = _hyena_core(mod_matmul(x_lat, sh1, sc1, w_in), *hy)
            x_lat = matmul_postnorm(z, w_out, x_lat, g1, lg0, lb0)
            if ctx_update:
                zc = _hyena_core(mod_matmul(x_ctx, cmod[0], cmod[1], w_in), *hy)
                x_ctx = matmul_postnorm(zc, w_out, x_ctx, cmod[2], lg0, lb0)
        elif kind == 1:
            w_qkv = at_w_qkv[slot].astype(BF16)
            w_out = at_w_out[slot].astype(BF16).reshape(N_Q_HEADS, HEAD_DIM, D)
            cos, sin = _axial_rope_tables(L)
            cos_t = jnp.tile(cos, (1, 128 // cos.shape[1]))
            sin_t = jnp.tile(jnp.concatenate([-sin, sin], axis=-1), (1, 128 // HEAD_DIM))
            qkv = mod_matmul(x_lat, sh1, sc1, w_qkv)
            q, k, v = qkv_prep(qkv, at_q_gain[slot], at_k_gain[slot], N_Q_HEADS, N_KV_HEADS, (cos_t, sin_t))
            kvc = mod_matmul(x_ctx, cmod[0], cmod[1], w_qkv[:, Q_COLS:])
            ck, cv = qkv_prep(kvc, None, at_k_gain[slot], 0, N_KV_HEADS, None)
            o = attention(q, ck, cv, k, v)
            x_lat = heads_matmul_postnorm(o, w_out, x_lat, g1, lg0, lb0)
        else:
            w_in = cm_w_in[slot].astype(BF16)
            w_out = cm_w_out[slot].astype(BF16)
            x_lat = gmlp_block(x_lat, sh1, sc1, w_in, cm_ln_g[slot][None], cm_ln_b[slot][None],
                               cm_w_s[slot].astype(BF16), cm_b_s[slot].T, w_out, g1, lg0, lb0)

        moe = (moe_router[i].T.astype(BF16), moe_w_gate, moe_w_up, moe_w_down, i)
        x_lat = postnorm(_expert_choice_ffn(x_lat, sh2, sc2, *moe), x_lat, g2, lg1, lb1)
        if ctx_update:
            x_ctx = postnorm(_expert_choice_ffn(x_ctx, cmod[3], cmod[4], *moe), x_ctx, cmod[5], lg1, lb1)
    return x_lat
```
